```python
import jax, jax.numpy as jnp
from jax import lax
import numpy as np

D_MODEL = 1024
BATCH = 8
SEQ = 4096
DEPTH = 2

CHUNK = 64
Q_BLOCK = 128
N_A_LAYERS = DEPTH // 2
N_B_LAYERS = DEPTH - N_A_LAYERS
CONV_WIDTH = 31
D_FF = 4 * D_MODEL
N_HEADS = 8
QK_NOPE_DIM = 128
QK_ROPE_DIM = 64
QK_HEAD_DIM = QK_NOPE_DIM + QK_ROPE_DIM
V_HEAD_DIM = 128
Q_LORA_RANK = 384
KV_LORA_RANK = 256
ROPE_BASE = 10000.0
RMS_EPS = 1e-6
LN_EPS = 1e-5

kernel_name = "yoco_conformer_conv_mla_hybrid"


def rms_norm(x, g):
    xf = x.astype(jnp.float32)
    y = xf * lax.rsqrt(jnp.mean(xf * xf, axis=-1, keepdims=True) + RMS_EPS)
    return (y * g.astype(jnp.float32)).astype(x.dtype)


def layer_norm(x, g, b):
    xf = x.astype(jnp.float32)
    mu = jnp.mean(xf, axis=-1, keepdims=True)
    xc = xf - mu
    var = jnp.mean(xc * xc, axis=-1, keepdims=True)
    y = xc * lax.rsqrt(var + LN_EPS) * g.astype(jnp.float32) + b.astype(jnp.float32)
    return y.astype(x.dtype)


def rope_tables(seq):
    inv = 1.0 / (ROPE_BASE ** (jnp.arange(0, QK_ROPE_DIM, 2, dtype=jnp.float32) / QK_ROPE_DIM))
    pos = jnp.arange(seq, dtype=jnp.float32)
    ang = pos[:, None] * inv[None, :]
    return jnp.cos(ang), jnp.sin(ang)


def apply_rope(x, cos, sin):
    xf = x.astype(jnp.float32)
    half = QK_ROPE_DIM // 2
    x1, x2 = xf[..., :half], xf[..., half:]
    out = jnp.concatenate([x1 * cos - x2 * sin, x1 * sin + x2 * cos], axis=-1)
    return out.astype(x.dtype)


def conformer_conv(h, w_pw1, b_pw1, w_dw, b_dw, ln_g, ln_b, w_pw2, b_pw2):
    u = h @ w_pw1 + b_pw1
    a, gate = jnp.split(u, 2, axis=-1)
    u = a * jax.nn.sigmoid(gate)
    u = jnp.pad(u, ((0, 0), (CONV_WIDTH - 1, 0), (0, 0)))
    u = lax.conv_general_dilated(u, w_dw[:, None, :], window_strides=(1,), padding='VALID',
                                 dimension_numbers=('NWC', 'WIO', 'NWC'),
                                 feature_group_count=D_MODEL) + b_dw
    u = jax.nn.silu(layer_norm(u, ln_g, ln_b))
    return u @ w_pw2 + b_pw2


def squared_relu_mlp(h, w1, w2):
    return jnp.square(jax.nn.relu(h @ w1)) @ w2


def mla_shared_kv(h, kv_in_g, w_dkv, kv_norm_g, w_kr, w_uk, w_uv, cos, sin):
    b, s, _ = h.shape
    hn = rms_norm(h, kv_in_g)
    c_kv = rms_norm(hn @ w_dkv, kv_norm_g)
    k_rope = apply_rope(hn @ w_kr, cos, sin)
    k_nope = (c_kv @ w_uk).reshape(b, s, N_HEADS, QK_NOPE_DIM)
    v = (c_kv @ w_uv).reshape(b, s, N_HEADS, V_HEAD_DIM)
    return k_nope, k_rope, v


def mla_attention(hn, w_dq, q_norm_g, w_uq, w_o, k_nope, k_rope, v, cos, sin):
    b, s, _ = hn.shape
    q = (rms_norm(hn @ w_dq, q_norm_g) @ w_uq).reshape(b, s, N_HEADS, QK_HEAD_DIM)
    q_nope = q[..., :QK_NOPE_DIM]
    q_rope = apply_rope(q[..., QK_NOPE_DIM:], cos[:, None, :], sin[:, None, :])
    scale = QK_HEAD_DIM ** -0.5
    key_chunk = jnp.arange(s) // CHUNK

    def block(i):
        start = i * Q_BLOCK
        qn = lax.dynamic_slice_in_dim(q_nope, start, Q_BLOCK, axis=1)
        qr = lax.dynamic_slice_in_dim(q_rope, start, Q_BLOCK, axis=1)
        sc = (jnp.einsum('bqhd,bkhd->bhqk', qn, k_nope, preferred_element_type=jnp.float32)
              + jnp.einsum('bqhr,bkr->bhqk', qr, k_rope, preferred_element_type=jnp.float32)) * scale
        q_chunk = (start + jnp.arange(Q_BLOCK)) // CHUNK
        mask = key_chunk[None, :] <= q_chunk[:, None]
        sc = jnp.where(mask[None, None], sc, -jnp.inf)
        p = jax.nn.softmax(sc, axis=-1).astype(v.dtype)
        return jnp.einsum('bhqk,bkhd->bqhd', p, v)

    o = lax.map(block, jnp.arange(s // Q_BLOCK))
    o = jnp.transpose(o, (1, 0, 2, 3, 4)).reshape(b, s, N_HEADS * V_HEAD_DIM)
    return o @ w_o


def setup_inputs(seed: int = 0) -> dict:
    key = jax.random.key(seed)
    ks = jax.random.split(key, 40)
    f32 = jnp.float32

    def nrm(k, shape, fan_in):
        return jax.random.normal(k, shape, f32) * (fan_in ** -0.5)

    def gain(k, shape):
        return 1.0 + 0.1 * jax.random.normal(k, shape, f32)

    def bias(k, shape):
        return 0.02 * jax.random.normal(k, shape, f32)

    D = D_MODEL
    return {
        "x": jax.random.normal(ks[0], (BATCH, SEQ, D), f32),
        "mix_pre_g": gain(ks[1], (DEPTH, D)),
        "mix_post_g": gain(ks[2], (DEPTH, D)),
        "ffn_pre_g": gain(ks[3], (DEPTH, D)),
        "ffn_post_g": gain(ks[4], (DEPTH, D)),
        "w_ff1": nrm(ks[5], (DEPTH, D, D_FF), D),
        "w_ff2": nrm(ks[6], (DEPTH, D_FF, D), D_FF),
        "conv_w_pw1": nrm(ks[7], (N_A_LAYERS, D, 2 * D), D),
        "conv_b_pw1": bias(ks[8], (N_A_LAYERS, 2 * D)),
        "conv_w_dw": nrm(ks[9], (N_A_LAYERS, CONV_WIDTH, D), CONV_WIDTH),
        "conv_b_dw": bias(ks[10], (N_A_LAYERS, D)),
        "conv_ln_g": gain(ks[11], (N_A_LAYERS, D)),
        "conv_ln_b": bias(ks[12], (N_A_LAYERS, D)),
        "conv_w_pw2": nrm(ks[13], (N_A_LAYERS, D, D), D),
        "conv_b_pw2": bias(ks[14], (N_A_LAYERS, D)),
        "mla_w_dq": nrm(ks[15], (N_B_LAYERS, D, Q_LORA_RANK), D),
        "mla_q_norm_g": gain(ks[16], (N_B_LAYERS, Q_LORA_RANK)),
        "mla_w_uq": nrm(ks[17], (N_B_LAYERS, Q_LORA_RANK, N_HEADS * QK_HEAD_DIM), Q_LORA_RANK),
        "mla_w_o": nrm(ks[18], (N_B_LAYERS, N_HEADS * V_HEAD_DIM, D), N_HEADS * V_HEAD_DIM),
        "kv_in_g": gain(ks[19], (D,)),
        "kv_w_dkv": nrm(ks[20], (D, KV_LORA_RANK), D),
        "kv_norm_g": gain(ks[21], (KV_LORA_RANK,)),
        "kv_w_kr": nrm(ks[22], (D, QK_ROPE_DIM), D),
        "kv_w_uk": nrm(ks[23], (KV_LORA_RANK, N_HEADS * QK_NOPE_DIM), KV_LORA_RANK),
        "kv_w_uv": nrm(ks[24], (KV_LORA_RANK, N_HEADS * V_HEAD_DIM), KV_LORA_RANK),
    }


def reference(x, mix_pre_g, mix_post_g, ffn_pre_g, ffn_post_g, w_ff1, w_ff2,
              conv_w_pw1, conv_b_pw1, conv_w_dw, conv_b_dw, conv_ln_g, conv_ln_b,
              conv_w_pw2, conv_b_pw2, mla_w_dq, mla_q_norm_g, mla_w_uq, mla_w_o,
              kv_in_g, kv_w_dkv, kv_norm_g, kv_w_kr, kv_w_uk, kv_w_uv):
    cos, sin = rope_tables(x.shape[1])
    h = x
    k_nope = k_rope = v = None
    for layer in range(DEPTH):
        if layer == N_A_LAYERS:
            k_nope, k_rope, v = mla_shared_kv(h, kv_in_g, kv_w_dkv, kv_norm_g, kv_w_kr,
                                              kv_w_uk, kv_w_uv, cos, sin)
        hn = rms_norm(h, mix_pre_g[layer])
        if layer < N_A_LAYERS:
            a = layer
            m = conformer_conv(hn, conv_w_pw1[a], conv_b_pw1[a], conv_w_dw[a], conv_b_dw[a],
                               conv_ln_g[a], conv_ln_b[a], conv_w_pw2[a], conv_b_pw2[a])
        else:
            bl = layer - N_A_LAYERS
            m = mla_attention(hn, mla_w_dq[bl], mla_q_norm_g[bl], mla_w_uq[bl], mla_w_o[bl],
                              k_nope, k_rope, v, cos, sin)
        h = h + rms_norm(m, mix_post_g[layer])
        f = squared_relu_mlp(rms_norm(h, ffn_pre_g[layer]), w_ff1[layer], w_ff2[layer])
        h = h + rms_norm(f, ffn_post_g[layer])
    return h
```

```python
import functools

import jax
import jax.numpy as jnp
from jax import lax
from jax.experimental import pallas as pl
from jax.experimental.pallas import tpu as pltpu

F32 = jnp.float32
BF16 = jnp.bfloat16

RMS_EPS = 1e-6
LN_EPS = 1e-5
ROPE_BASE = 10000.0
CHUNK = 64
CONV_WIDTH = 31
N_HEADS = 8
NOPE = 128
ROPE = 64
V_DIM = 128
QK_DIM = NOPE + ROPE
KPAD = 256

SUBLANES = 8
HALO = 32
CONV_ROWS = 32
CONV_LANES = 256
FFN_COLS = 1024
TQ = 256
TK = 256
VMEM_LIMIT = 56 * 1024 * 1024


def _rms_scale(x):
    return x * lax.rsqrt(jnp.mean(x * x, axis=-1, keepdims=True) + RMS_EPS)


def _ffn(h, g_pre, w1_ref, w2_ref, g_post):
    hn = (_rms_scale(h) * g_pre).astype(BF16)
    d_ff = w1_ref.shape[1]
    acc = None
    for c in range(d_ff // FFN_COLS):
        cols = slice(c * FFN_COLS, (c + 1) * FFN_COLS)
        a = jnp.maximum(jnp.dot(hn, w1_ref[:, cols], preferred_element_type=F32), 0.0)
        t = jnp.dot((a * a).astype(BF16), w2_ref[cols, :], preferred_element_type=F32)
        acc = t if acc is None else acc + t
    return h + _rms_scale(acc) * g_post


def _conv_mixer_kernel(x_ref, gpre_ref, wpw1_ref, bpw1_ref, wdw_ref, bdw_ref, lng_ref, lnb_ref,
                       wpw2_ref, bpw2_ref, gpost_ref, o_ref, buf_ref, cv_ref):
    tm, d = x_ref.shape[1], x_ref.shape[2]

    @pl.when(pl.program_id(1) == 0)
    def _zero_history():
        buf_ref[0:HALO, :] = jnp.zeros((HALO, d), F32)

    x = x_ref[0]
    hn = (_rms_scale(x) * gpre_ref[...]).astype(BF16)
    u = jnp.dot(hn, wpw1_ref[...], preferred_element_type=F32) + bpw1_ref[...]
    buf_ref[HALO:HALO + tm, :] = u[:, :d] * jax.nn.sigmoid(u[:, d:])

    first_off = HALO - (CONV_WIDTH - 1)

    def conv_rows(i, carry):
        r0 = pl.multiple_of(i * CONV_ROWS, CONV_ROWS)
        for c in range(d // CONV_LANES):
            lanes = slice(c * CONV_LANES, (c + 1) * CONV_LANES)
            out = jnp.broadcast_to(bdw_ref[:, lanes], (CONV_ROWS, CONV_LANES))
            for r in range(SUBLANES):
                rows = CONV_ROWS + (SUBLANES if r else 0)
                group = None
                for off in range(r, HALO + 1, SUBLANES):
                    if off < first_off:
                        continue
                    k = off - first_off
                    term = wdw_ref[k:k + 1, lanes] * buf_ref[pl.ds(r0 + (off - r), rows), lanes]
                    group = term if group is None else group + term
                out = out + group[r:r + CONV_ROWS]
            cv_ref[pl.ds(r0, CONV_ROWS), lanes] = out
        return carry

    lax.fori_loop(0, tm // CONV_ROWS, conv_rows, 0)
    buf_ref[0:HALO, :] = buf_ref[tm:tm + HALO, :]

    cv = cv_ref[...]
    xc = cv - jnp.mean(cv, axis=-1, keepdims=True)
    y = xc * lax.rsqrt(jnp.mean(xc * xc, axis=-1, keepdims=True) + LN_EPS) * lng_ref[...] + lnb_ref[...]
    y = (y * jax.nn.sigmoid(y)).astype(BF16)
    m = jnp.dot(y, wpw2_ref[...], preferred_element_type=F32) + bpw2_ref[...]
    o_ref[0] = x + _rms_scale(m) * gpost_ref[...]


def _ffn_proj_kernel(h_ref, cs_ref, gpre_ref, w1_ref, w2_ref, gpost_ref,
                     gkv_ref, wdkv_ref, gckv_ref, wuk_ref, wuv_ref,
                     gq_ref, wdq_ref, gcq_ref, wuq_ref,
                     h_out_ref, k_ref, vt_ref, qt_ref):
    tm = h_ref.shape[1]
    h = _ffn(h_ref[0], gpre_ref[...], w1_ref, w2_ref, gpost_ref[...])
    h_out_ref[0] = h

    hs = _rms_scale(h)
    cs = cs_ref[...]
    c_rank = gckv_ref.shape[1]
    lane = lax.broadcasted_iota(jnp.int32, (tm, 2 * ROPE), 1)

    ckv = jnp.dot((hs * gkv_ref[...]).astype(BF16), wdkv_ref[...], preferred_element_type=F32)
    c_kv = (_rms_scale(ckv[:, :c_rank]) * gckv_ref[...]).astype(BF16)
    kr = ckv[:, c_rank:] * cs
    kr = kr + pltpu.roll(kr, ROPE, 1)
    kr = jnp.where(lane < ROPE, kr, 0.0).astype(BF16)
    kn = jnp.dot(c_kv, wuk_ref[...], preferred_element_type=F32)
    v = jnp.dot(c_kv, wuv_ref[...], preferred_element_type=F32)
    for hd in range(N_HEADS):
        k_ref[0, hd, :, 0:NOPE] = kn[:, hd * NOPE:(hd + 1) * NOPE].astype(BF16)
        k_ref[0, hd, :, NOPE:KPAD] = kr
        for blk in range(tm // TK):
            vh = v[blk * TK:(blk + 1) * TK, hd * V_DIM:(hd + 1) * V_DIM]
            vt_ref[0, hd, blk] = vh.T.astype(BF16)

    cq = jnp.dot((hs * gq_ref[...]).astype(BF16), wdq_ref[...], preferred_element_type=F32)
    cq = (_rms_scale(cq) * gcq_ref[...]).astype(BF16)
    q = jnp.dot(cq, wuq_ref[...], preferred_element_type=F32) * (QK_DIM ** -0.5)
    for hd in range(N_HEADS):
        qn = q[:, hd * KPAD:hd * KPAD + NOPE]
        qr = q[:, hd * KPAD + NOPE:(hd + 1) * KPAD] * cs
        qr = qr + pltpu.roll(qr, ROPE, 1)
        for blk in range(tm // TQ):
            rows = slice(blk * TQ, (blk + 1) * TQ)
            qt_ref[0, hd, blk, 0:NOPE, :] = qn[rows].T.astype(BF16)
            qt_ref[0, hd, blk, NOPE:KPAD, :] = qr[rows].T.astype(BF16)


def _attn_kernel(qt_ref, k_ref, vt_ref, o_ref):
    qi = pl.program_id(2)
    n_g = qt_ref.shape[1]
    key_chunk = lax.broadcasted_iota(jnp.int32, (TK, TQ), 0) // CHUNK
    qry_chunk = lax.broadcasted_iota(jnp.int32, (TK, TQ), 1) // CHUNK
    diag_mask = key_chunk <= qry_chunk

    def scores(g, j):
        k = k_ref[0, g, pl.ds(pl.multiple_of(j * TK, TK), TK), :]
        return jnp.dot(k, qt_ref[0, g, 0], preferred_element_type=F32)

    state = []
    for g in range(n_g):
        s = jnp.where(diag_mask, scores(g, qi), -jnp.inf)
        m = jnp.max(s, axis=0, keepdims=True)
        p = jnp.exp(s - m)
        l = jnp.sum(p, axis=0, keepdims=True)
        acc = jnp.dot(vt_ref[0, g, qi], p.astype(BF16), preferred_element_type=F32)
        state += [m, l, acc]

    def kv_step(j, carry):
        new = []
        for g in range(n_g):
            m, l, acc = carry[3 * g:3 * g + 3]
            s = scores(g, j)
            m_new = jnp.maximum(m, jnp.max(s, axis=0, keepdims=True))
            alpha = jnp.exp(m - m_new)
            p = jnp.exp(s - m_new)
            l = alpha * l + jnp.sum(p, axis=0, keepdims=True)
            acc = alpha * acc + jnp.dot(vt_ref[0, g, j], p.astype(BF16), preferred_element_type=F32)
            new += [m_new, l, acc]
        return tuple(new)

    state = lax.fori_loop(0, qi, kv_step, tuple(state))
    for g in range(n_g):
        _, l, acc = state[3 * g:3 * g + 3]
        o_ref[0, :, g * V_DIM:(g + 1) * V_DIM] = (acc * (1.0 / l)).T.astype(BF16)


def _out_ffn_kernel(o_ref, h_ref, wo_ref, gmix_ref, gpre_ref, w1_ref, w2_ref, gpost_ref, out_ref):
    m = jnp.dot(o_ref[0], wo_ref[...], preferred_element_type=F32)
    h = h_ref[0] + _rms_scale(m) * gmix_ref[...]
    out_ref[0] = _ffn(h, gpre_ref[...], w1_ref, w2_ref, gpost_ref[...])


def _const_spec(shape):
    zeros = (0,) * len(shape)
    return pl.BlockSpec(shape, lambda *_: zeros, pipeline_mode=pl.Buffered(1))


def _row(v):
    return v.reshape(1, -1).astype(F32)


def _rot_half_cols(w):
    half = w.shape[-1] // 2
    return jnp.concatenate([-w[..., half:], w[..., :half]], axis=-1)


def _params(n_grid_axes):
    return pltpu.CompilerParams(dimension_semantics=("arbitrary",) * n_grid_axes,
                                vmem_limit_bytes=VMEM_LIMIT)


def kernel(x, mix_pre_g, mix_post_g, ffn_pre_g, ffn_post_g, w_ff1, w_ff2, conv_w_pw1, conv_b_pw1, conv_w_dw, conv_b_dw, conv_ln_g, conv_ln_b, conv_w_pw2, conv_b_pw2, mla_w_dq, mla_q_norm_g, mla_w_uq, mla_w_o, kv_in_g, kv_w_dkv, kv_norm_g, kv_w_kr, kv_w_uk, kv_w_uv):
    b, s, d = x.shape
    assert mix_pre_g.shape[0] == 2 and conv_w_pw1.shape[0] == 1 and mla_w_dq.shape[0] == 1
    assert conv_w_dw.shape[1] == CONV_WIDTH
    tm = min(512, s)
    assert s % tm == 0 and tm % TQ == 0 and TQ == TK and d % CONV_LANES == 0
    n_s = s // tm
    d_ff = w_ff1.shape[2]
    q_rank = mla_w_dq.shape[2]
    c_rank = kv_w_dkv.shape[1]

    tile = pl.BlockSpec((1, tm, d), lambda i, j: (i, j, 0))

    h1 = pl.pallas_call(
        _conv_mixer_kernel,
        grid=(b, n_s),
        in_specs=[tile, _const_spec((1, d)), _const_spec((d, 2 * d)), _const_spec((1, 2 * d)),
                  _const_spec((CONV_WIDTH, d)), _const_spec((1, d)), _const_spec((1, d)),
                  _const_spec((1, d)), _const_spec((d, d)), _const_spec((1, d)), _const_spec((1, d))],
        out_specs=tile,
        out_shape=jax.ShapeDtypeStruct((b, s, d), F32),
        scratch_shapes=[pltpu.VMEM((tm + HALO, d), F32), pltpu.VMEM((tm, d), F32)],
        compiler_params=_params(2),
        name="conv_mixer",
    )(x, _row(mix_pre_g[0]), conv_w_pw1[0].astype(BF16), _row(conv_b_pw1[0]),
      conv_w_dw[0].astype(F32), _row(conv_b_dw[0]), _row(conv_ln_g[0]), _row(conv_ln_b[0]),
      conv_w_pw2[0].astype(BF16), _row(conv_b_pw2[0]), _row(mix_post_g[0]))

    inv = 1.0 / (ROPE_BASE ** (jnp.arange(0, ROPE, 2, dtype=F32) / ROPE))
    ang = jnp.arange(s, dtype=F32)[:, None] * inv[None, :]
    cs = jnp.concatenate([jnp.cos(ang), jnp.cos(ang), jnp.sin(ang), jnp.sin(ang)], axis=-1)

    w_dkv_kr = jnp.concatenate([kv_w_dkv, kv_w_kr, _rot_half_cols(kv_w_kr)], axis=-1).astype(BF16)
    w_uq = mla_w_uq[0].reshape(q_rank, N_HEADS, QK_DIM)
    w_uq = jnp.concatenate([w_uq, _rot_half_cols(w_uq[..., NOPE:])], axis=-1)
    w_uq = w_uq.reshape(q_rank, N_HEADS * KPAD).astype(BF16)

    h2, k_cat, v_t, q_t = pl.pallas_call(
        _ffn_proj_kernel,
        grid=(b, n_s),
        in_specs=[tile, pl.BlockSpec((tm, 2 * ROPE), lambda i, j: (j, 0)),
                  _const_spec((1, d)), _const_spec((d, d_ff)), _const_spec((d_ff, d)), _const_spec((1, d)),
                  _const_spec((1, d)), _const_spec((d, c_rank + 2 * ROPE)), _const_spec((1, c_rank)),
                  _const_spec((c_rank, N_HEADS * NOPE)), _const_spec((c_rank, N_HEADS * V_DIM)),
                  _const_spec((1, d)), _const_spec((d, q_rank)), _const_spec((1, q_rank)),
                  _const_spec((q_rank, N_HEADS * KPAD))],
        out_specs=[tile,
                   pl.BlockSpec((1, N_HEADS, tm, KPAD), lambda i, j: (i, 0, j, 0)),
                   pl.BlockSpec((1, N_HEADS, tm // TK, V_DIM, TK), lambda i, j: (i, 0, j, 0, 0)),
                   pl.BlockSpec((1, N_HEADS, tm // TQ, KPAD, TQ), lambda i, j: (i, 0, j, 0, 0))],
        out_shape=[jax.ShapeDtypeStruct((b, s, d), F32),
                   jax.ShapeDtypeStruct((b, N_HEADS, s, KPAD), BF16),
                   jax.ShapeDtypeStruct((b, N_HEADS, s // TK, V_DIM, TK), BF16),
                   jax.ShapeDtypeStruct((b, N_HEADS, s // TQ, KPAD, TQ), BF16)],
        compiler_params=_params(2),
        name="ffn_proj",
    )(h1, cs, _row(ffn_pre_g[0]), w_ff1[0].astype(BF16), w_ff2[0].astype(BF16), _row(ffn_post_g[0]),
      _row(kv_in_g), w_dkv_kr, _row(kv_norm_g), kv_w_uk.astype(BF16), kv_w_uv.astype(BF16),
      _row(mix_pre_g[1]), mla_w_dq[0].astype(BF16), _row(mla_q_norm_g[0]), w_uq)

    n_g = 2
    o = pl.pallas_call(
        _attn_kernel,
        grid=(b, N_HEADS // n_g, s // TQ),
        in_specs=[pl.BlockSpec((1, n_g, 1, KPAD, TQ), lambda i, g, q: (i, g, q, 0, 0)),
                  pl.BlockSpec((1, n_g, s, KPAD), lambda i, g, q: (i, g, 0, 0)),
                  pl.BlockSpec((1, n_g, s // TK, V_DIM, TK), lambda i, g, q: (i, g, 0, 0, 0))],
        out_specs=pl.BlockSpec((1, TQ, n_g * V_DIM), lambda i, g, q: (i, q, g)),
        out_shape=jax.ShapeDtypeStruct((b, s, N_HEADS * V_DIM), BF16),
        compiler_params=_params(3),
        name="attention",
    )(q_t, k_cat, v_t)

    return pl.pallas_call(
        _out_ffn_kernel,
        grid=(b, n_s),
        in_specs=[pl.BlockSpec((1, tm, N_HEADS * V_DIM), lambda i, j: (i, j, 0)), tile,
                  _const_spec((N_HEADS * V_DIM, d)), _const_spec((1, d)), _const_spec((1, d)),
                  _const_spec((d, d_ff)), _const_spec((d_ff, d)), _const_spec((1, d))],
        out_specs=tile,
        out_shape=jax.ShapeDtypeStruct((b, s, d), F32),
        compiler_params=_params(2),
        name="out_ffn",
    )(o, h2, mla_w_o[0].astype(BF16), _row(mix_post_g[1]), _row(ffn_pre_g[1]),
      w_ff1[1].astype(BF16), w_ff2[1].astype(BF16), _row(ffn_post_g[1]))
```

```python
import functools

import jax
import jax.numpy as jnp
from jax import lax
from jax.experimental import pallas as pl
from jax.experimental.pallas import tpu as pltpu

F32 = jnp.float32
BF16 = jnp.bfloat16

RMS_EPS = 1e-6
LN_EPS = 1e-5
ROPE_BASE = 10000.0
CHUNK = 64
CONV_WIDTH = 31
N_HEADS = 8
NOPE = 128
ROPE = 64
V_DIM = 128
QK_DIM = NOPE + ROPE
KPAD = 256
LOG2_E = 1.4426950408889634
Q_SCALE = QK_DIM ** -0.5 * LOG2_E

SUBLANES = 8
HALO = 32
CONV_ROWS = 32
CONV_LANES = 256
FFN_COLS = 1024
TQ = 256
TK = 256
VMEM_LIMIT = 56 * 1024 * 1024


def _rms_scale(x):
    return x * lax.rsqrt(jnp.mean(x * x, axis=-1, keepdims=True) + RMS_EPS)


def _ffn(h, g_pre, w1_ref, w2_ref, g_post):
    hn = (_rms_scale(h) * g_pre).astype(BF16)
    d_ff = w1_ref.shape[1]
    acc = None
    for c in range(d_ff // FFN_COLS):
        cols = slice(c * FFN_COLS, (c + 1) * FFN_COLS)
        a = jnp.maximum(jnp.dot(hn, w1_ref[:, cols], preferred_element_type=F32), 0.0)
        t = jnp.dot((a * a).astype(BF16), w2_ref[cols, :], preferred_element_type=F32)
        acc = t if acc is None else acc + t
    return h + _rms_scale(acc) * g_post


def _conv_mixer_kernel(x_ref, gpre_ref, wpw1_ref, bpw1_ref, wdw_ref, bdw_ref, lng_ref, lnb_ref,
                       wpw2_ref, bpw2_ref, gpost_ref, o_ref, buf_ref, cv_ref):
    tm, d = x_ref.shape[1], x_ref.shape[2]

    @pl.when(pl.program_id(1) == 0)
    def _zero_history():
        buf_ref[0:HALO, :] = jnp.zeros((HALO, d), F32)

    x = x_ref[0]
    hn = (_rms_scale(x) * gpre_ref[...]).astype(BF16)
    u = jnp.dot(hn, wpw1_ref[...], preferred_element_type=F32) + bpw1_ref[...]
    buf_ref[HALO:HALO + tm, :] = u[:, :d] * jax.nn.sigmoid(u[:, d:])

    first_off = HALO - (CONV_WIDTH - 1)

    def conv_rows(i, carry):
        r0 = pl.multiple_of(i * CONV_ROWS, CONV_ROWS)
        for c in range(d // CONV_LANES):
            lanes = slice(c * CONV_LANES, (c + 1) * CONV_LANES)
            out = jnp.broadcast_to(bdw_ref[:, lanes], (CONV_ROWS, CONV_LANES))
            for r in range(SUBLANES):
                rows = CONV_ROWS + (SUBLANES if r else 0)
                group = None
                for off in range(r, HALO + 1, SUBLANES):
                    if off < first_off:
                        continue
                    k = off - first_off
                    term = wdw_ref[k:k + 1, lanes] * buf_ref[pl.ds(r0 + (off - r), rows), lanes]
                    group = term if group is None else group + term
                out = out + group[r:r + CONV_ROWS]
            cv_ref[pl.ds(r0, CONV_ROWS), lanes] = out
        return carry

    lax.fori_loop(0, tm // CONV_ROWS, conv_rows, 0)
    buf_ref[0:HALO, :] = buf_ref[tm:tm + HALO, :]

    cv = cv_ref[...]
    xc = cv - jnp.mean(cv, axis=-1, keepdims=True)
    y = xc * lax.rsqrt(jnp.mean(xc * xc, axis=-1, keepdims=True) + LN_EPS) * lng_ref[...] + lnb_ref[...]
    y = (y * jax.nn.sigmoid(y)).astype(BF16)
    m = jnp.dot(y, wpw2_ref[...], preferred_element_type=F32) + bpw2_ref[...]
    o_ref[0] = x + _rms_scale(m) * gpost_ref[...]


def _ffn_proj_kernel(h_ref, cs_ref, gpre_ref, w1_ref, w2_ref, gpost_ref,
                     gkv_ref, wdkv_ref, gckv_ref, wuk_ref, wuv_ref,
                     gq_ref, wdq_ref, gcq_ref, wuq_ref,
                     h_out_ref, k_ref, vt_ref, qt_ref):
    tm = h_ref.shape[1]
    h = _ffn(h_ref[0], gpre_ref[...], w1_ref, w2_ref, gpost_ref[...])
    h_out_ref[0] = h

    hs = _rms_scale(h)
    cs = cs_ref[...]
    c_rank = gckv_ref.shape[1]
    lane = lax.broadcasted_iota(jnp.int32, (tm, 2 * ROPE), 1)

    ckv = jnp.dot((hs * gkv_ref[...]).astype(BF16), wdkv_ref[...], preferred_element_type=F32)
    c_kv = (_rms_scale(ckv[:, :c_rank]) * gckv_ref[...]).astype(BF16)
    kr = ckv[:, c_rank:] * cs
    kr = kr + pltpu.roll(kr, ROPE, 1)
    kr = jnp.where(lane < ROPE, kr, 0.0).astype(BF16)
    kn = jnp.dot(c_kv, wuk_ref[...], preferred_element_type=F32)
    v = jnp.dot(c_kv, wuv_ref[...], preferred_element_type=F32)
    for hd in range(N_HEADS):
        k_ref[0, hd, :, 0:NOPE] = kn[:, hd * NOPE:(hd + 1) * NOPE].astype(BF16)
        k_ref[0, hd, :, NOPE:KPAD] = kr
        for blk in range(tm // TK):
            vh = v[blk * TK:(blk + 1) * TK, hd * V_DIM:(hd + 1) * V_DIM]
            vt_ref[0, hd, blk] = vh.T.astype(BF16)

    cq = jnp.dot((hs * gq_ref[...]).astype(BF16), wdq_ref[...], preferred_element_type=F32)
    cq = (_rms_scale(cq) * gcq_ref[...]).astype(BF16)
    q = jnp.dot(cq, wuq_ref[...], preferred_element_type=F32) * Q_SCALE
    for hd in range(N_HEADS):
        qn = q[:, hd * KPAD:hd * KPAD + NOPE]
        qr = q[:, hd * KPAD + NOPE:(hd + 1) * KPAD] * cs
        qr = qr + pltpu.roll(qr, ROPE, 1)
        for blk in range(tm // TQ):
            rows = slice(blk * TQ, (blk + 1) * TQ)
            qt_ref[0, hd, blk, 0:NOPE, :] = qn[rows].T.astype(BF16)
            qt_ref[0, hd, blk, NOPE:KPAD, :] = qr[rows].T.astype(BF16)


def _attn_kernel(qtab_ref, jtab_ref, qt_ref, k_ref, vt_ref, bias_ref, ot_ref,
                 s_ref, mx_ref, p_ref, al_ref, m_ref, l_ref, acc_ref):
    nq = qt_ref.shape[2]
    n_items = qtab_ref.shape[0]
    m_ref[...] = jnp.full(m_ref.shape, -jnp.inf, F32)
    l_ref[...] = jnp.zeros(l_ref.shape, F32)
    acc_ref[...] = jnp.zeros(acc_ref.shape, F32)

    def qk(t, slot):
        qi, j = qtab_ref[t], jtab_ref[t]
        k = k_ref[0, 0, pl.ds(pl.multiple_of(j * TK, TK), TK), :]
        s = jnp.dot(k, qt_ref[0, 0, qi], preferred_element_type=F32)
        s = s + bias_ref[(qi == j).astype(jnp.int32)]
        s_ref[slot] = s
        mx_ref[slot] = jnp.max(s, axis=0, keepdims=True)

    def softmax(t, slot):
        qi = qtab_ref[t]
        m_old = m_ref[qi]
        m_new = jnp.maximum(m_old, mx_ref[slot])
        alpha = jnp.exp2(m_old - m_new)
        p = jnp.exp2(s_ref[slot] - m_new)
        l_ref[qi] = alpha * l_ref[qi] + jnp.sum(p, axis=0, keepdims=True)
        m_ref[qi] = m_new
        al_ref[slot] = alpha
        p_ref[slot] = p.astype(BF16)

    def pv(t, slot):
        qi, j = qtab_ref[t], jtab_ref[t]
        pv_t = jnp.dot(vt_ref[0, 0, j], p_ref[slot], preferred_element_type=F32)
        acc_ref[qi] = al_ref[slot] * acc_ref[qi] + pv_t

    qk(0, 0)
    qk(1, 1)
    softmax(0, 0)

    def step_pair(u, carry):
        t = 2 + 2 * u
        qk(t, 0)
        pv(t - 2, 0)
        softmax(t - 1, 1)
        qk(t + 1, 1)
        pv(t - 1, 1)
        softmax(t, 0)
        return carry

    lax.fori_loop(0, (n_items - 2) // 2, step_pair, 0)
    pv(n_items - 2, 0)
    softmax(n_items - 1, 1)
    pv(n_items - 1, 1)

    def finish(qi, carry):
        ot_ref[0, 0, qi] = (acc_ref[qi] * (1.0 / l_ref[qi])).astype(BF16)
        return carry

    lax.fori_loop(0, nq, finish, 0)


def _out_ffn_kernel(ot_ref, h_ref, wo_ref, gmix_ref, gpre_ref, w1_ref, w2_ref, gpost_ref, out_ref, o_scr):
    for hd in range(ot_ref.shape[1]):
        for blk in range(ot_ref.shape[2]):
            o_t = ot_ref[0, hd, blk].astype(F32)
            o_scr[blk * TQ:(blk + 1) * TQ, hd * V_DIM:(hd + 1) * V_DIM] = o_t.T.astype(BF16)
    m = jnp.dot(o_scr[...], wo_ref[...], preferred_element_type=F32)
    h = h_ref[0] + _rms_scale(m) * gmix_ref[...]
    out_ref[0] = _ffn(h, gpre_ref[...], w1_ref, w2_ref, gpost_ref[...])


def _const_spec(shape):
    zeros = (0,) * len(shape)
    return pl.BlockSpec(shape, lambda *_: zeros, pipeline_mode=pl.Buffered(1))


def _row(v):
    return v.reshape(1, -1).astype(F32)


def _rot_half_cols(w):
    half = w.shape[-1] // 2
    return jnp.concatenate([-w[..., half:], w[..., :half]], axis=-1)


def _params(n_grid_axes):
    return pltpu.CompilerParams(dimension_semantics=("arbitrary",) * n_grid_axes,
                                vmem_limit_bytes=VMEM_LIMIT)


def kernel(x, mix_pre_g, mix_post_g, ffn_pre_g, ffn_post_g, w_ff1, w_ff2, conv_w_pw1, conv_b_pw1, conv_w_dw, conv_b_dw, conv_ln_g, conv_ln_b, conv_w_pw2, conv_b_pw2, mla_w_dq, mla_q_norm_g, mla_w_uq, mla_w_o, kv_in_g, kv_w_dkv, kv_norm_g, kv_w_kr, kv_w_uk, kv_w_uv):
    b, s, d = x.shape
    assert mix_pre_g.shape[0] == 2 and conv_w_pw1.shape[0] == 1 and mla_w_dq.shape[0] == 1
    assert conv_w_dw.shape[1] == CONV_WIDTH
    tm = min(512, s)
    assert s % tm == 0 and tm % TQ == 0 and TQ == TK and d % CONV_LANES == 0
    n_s = s // tm
    d_ff = w_ff1.shape[2]
    q_rank = mla_w_dq.shape[2]
    c_rank = kv_w_dkv.shape[1]

    tile = pl.BlockSpec((1, tm, d), lambda i, j: (i, j, 0))

    h1 = pl.pallas_call(
        _conv_mixer_kernel,
        grid=(b, n_s),
        in_specs=[tile, _const_spec((1, d)), _const_spec((d, 2 * d)), _const_spec((1, 2 * d)),
                  _const_spec((CONV_WIDTH, d)), _const_spec((1, d)), _const_spec((1, d)),
                  _const_spec((1, d)), _const_spec((d, d)), _const_spec((1, d)), _const_spec((1, d))],
        out_specs=tile,
        out_shape=jax.ShapeDtypeStruct((b, s, d), F32),
        scratch_shapes=[pltpu.VMEM((tm + HALO, d), F32), pltpu.VMEM((tm, d), F32)],
        compiler_params=_params(2),
        name="conv_mixer",
    )(x, _row(mix_pre_g[0]), conv_w_pw1[0].astype(BF16), _row(conv_b_pw1[0]),
      conv_w_dw[0].astype(F32), _row(conv_b_dw[0]), _row(conv_ln_g[0]), _row(conv_ln_b[0]),
      conv_w_pw2[0].astype(BF16), _row(conv_b_pw2[0]), _row(mix_post_g[0]))

    inv = 1.0 / (ROPE_BASE ** (jnp.arange(0, ROPE, 2, dtype=F32) / ROPE))
    ang = jnp.arange(s, dtype=F32)[:, None] * inv[None, :]
    cs = jnp.concatenate([jnp.cos(ang), jnp.cos(ang), jnp.sin(ang), jnp.sin(ang)], axis=-1)

    w_dkv_kr = jnp.concatenate([kv_w_dkv, kv_w_kr, _rot_half_cols(kv_w_kr)], axis=-1).astype(BF16)
    w_uq = mla_w_uq[0].reshape(q_rank, N_HEADS, QK_DIM)
    w_uq = jnp.concatenate([w_uq, _rot_half_cols(w_uq[..., NOPE:])], axis=-1)
    w_uq = w_uq.reshape(q_rank, N_HEADS * KPAD).astype(BF16)

    h2, k_cat, v_t, q_t = pl.pallas_call(
        _ffn_proj_kernel,
        grid=(b, n_s),
        in_specs=[tile, pl.BlockSpec((tm, 2 * ROPE), lambda i, j: (j, 0)),
                  _const_spec((1, d)), _const_spec((d, d_ff)), _const_spec((d_ff, d)), _const_spec((1, d)),
                  _const_spec((1, d)), _const_spec((d, c_rank + 2 * ROPE)), _const_spec((1, c_rank)),
                  _const_spec((c_rank, N_HEADS * NOPE)), _const_spec((c_rank, N_HEADS * V_DIM)),
                  _const_spec((1, d)), _const_spec((d, q_rank)), _const_spec((1, q_rank)),
                  _const_spec((q_rank, N_HEADS * KPAD))],
        out_specs=[tile,
                   pl.BlockSpec((1, N_HEADS, tm, KPAD), lambda i, j: (i, 0, j, 0)),
                   pl.BlockSpec((1, N_HEADS, tm // TK, V_DIM, TK), lambda i, j: (i, 0, j, 0, 0)),
                   pl.BlockSpec((1, N_HEADS, tm // TQ, KPAD, TQ), lambda i, j: (i, 0, j, 0, 0))],
        out_shape=[jax.ShapeDtypeStruct((b, s, d), F32),
                   jax.ShapeDtypeStruct((b, N_HEADS, s, KPAD), BF16),
                   jax.ShapeDtypeStruct((b, N_HEADS, s // TK, V_DIM, TK), BF16),
                   jax.ShapeDtypeStruct((b, N_HEADS, s // TQ, KPAD, TQ), BF16)],
        compiler_params=_params(2),
        name="ffn_proj",
    )(h1, cs, _row(ffn_pre_g[0]), w_ff1[0].astype(BF16), w_ff2[0].astype(BF16), _row(ffn_post_g[0]),
      _row(kv_in_g), w_dkv_kr, _row(kv_norm_g), kv_w_uk.astype(BF16), kv_w_uv.astype(BF16),
      _row(mix_pre_g[1]), mla_w_dq[0].astype(BF16), _row(mla_q_norm_g[0]), w_uq)

    nq = s // TQ
    items = [(qi, j) for qi in range(nq) for j in range(qi, -1, -1)]
    assert len(items) % 2 == 0 and len(items) >= 2
    qtab = jnp.asarray([qi for qi, _ in items], jnp.int32)
    jtab = jnp.asarray([j for _, j in items], jnp.int32)
    key_chunk = lax.broadcasted_iota(jnp.int32, (TK, TQ), 0) // CHUNK
    qry_chunk = lax.broadcasted_iota(jnp.int32, (TK, TQ), 1) // CHUNK
    bias = jnp.stack([jnp.zeros((TK, TQ), F32), jnp.where(key_chunk <= qry_chunk, 0.0, -jnp.inf)])
    o_t = pl.pallas_call(
        _attn_kernel,
        grid_spec=pltpu.PrefetchScalarGridSpec(
            num_scalar_prefetch=2,
            grid=(b, N_HEADS),
            in_specs=[pl.BlockSpec((1, 1, nq, KPAD, TQ), lambda i, h, *_: (i, h, 0, 0, 0)),
                      pl.BlockSpec((1, 1, s, KPAD), lambda i, h, *_: (i, h, 0, 0)),
                      pl.BlockSpec((1, 1, s // TK, V_DIM, TK), lambda i, h, *_: (i, h, 0, 0, 0)),
                      _const_spec((2, TK, TQ))],
            out_specs=pl.BlockSpec((1, 1, nq, V_DIM, TQ), lambda i, h, *_: (i, h, 0, 0, 0)),
            scratch_shapes=[pltpu.VMEM((2, TK, TQ), F32), pltpu.VMEM((2, 1, TQ), F32),
                            pltpu.VMEM((2, TK, TQ), BF16), pltpu.VMEM((2, 1, TQ), F32),
                            pltpu.VMEM((nq, 1, TQ), F32), pltpu.VMEM((nq, 1, TQ), F32),
                            pltpu.VMEM((nq, V_DIM, TQ), F32)]),
        out_shape=jax.ShapeDtypeStruct((b, N_HEADS, nq, V_DIM, TQ), BF16),
        compiler_params=_params(2),
        name="attention",
    )(qtab, jtab, q_t, k_cat, v_t, bias)

    return pl.pallas_call(
        _out_ffn_kernel,
        grid=(b, n_s),
        in_specs=[pl.BlockSpec((1, N_HEADS, tm // TQ, V_DIM, TQ), lambda i, j: (i, 0, j, 0, 0)), tile,
                  _const_spec((N_HEADS * V_DIM, d)), _const_spec((1, d)), _const_spec((1, d)),
                  _const_spec((d, d_ff)), _const_spec((d_ff, d)), _const_spec((1, d))],
        out_specs=tile,
        out_shape=jax.ShapeDtypeStruct((b, s, d), F32),
        scratch_shapes=[pltpu.VMEM((tm, N_HEADS * V_DIM), BF16)],
        compiler_params=_params(2),
        name="out_ffn",
    )(o_t, h2, mla_w_o[0].astype(BF16), _row(mix_post_g[1]), _row(ffn_pre_g[1]),
      w_ff1[1].astype(BF16), w_ff2[1].astype(BF16), _row(ffn_post_g[1]))
```

```python
import functools

import jax
import jax.numpy as jnp
from jax import lax
from jax.experimental import pallas as pl
from jax.experimental.pallas import tpu as pltpu

F32 = jnp.float32
BF16 = jnp.bfloat16

RMS_EPS = 1e-6
LN_EPS = 1e-5
ROPE_BASE = 10000.0
CHUNK = 64
CONV_WIDTH = 31
N_HEADS = 8
NOPE = 128
ROPE = 64
V_DIM = 128
QK_DIM = NOPE + ROPE
KPAD = 256
LOG2_E = 1.4426950408889634
Q_SCALE = QK_DIM ** -0.5 * LOG2_E

SUBLANES = 8
HALO = 32
CONV_ROWS = 32
CONV_LANES = 256
FFN_COLS = 1024
TQ = 256
TK = 256
ATTN_GROUP = 4
VMEM_LIMIT = 56 * 1024 * 1024


def _rms_scale(x):
    return x * lax.rsqrt(jnp.mean(x * x, axis=-1, keepdims=True) + RMS_EPS)


def _ffn(h, g_pre, w1_ref, w2_ref, g_post):
    hn = (_rms_scale(h) * g_pre).astype(BF16)
    d_ff = w1_ref.shape[1]
    acc = None
    for c in range(d_ff // FFN_COLS):
        cols = slice(c * FFN_COLS, (c + 1) * FFN_COLS)
        a = jnp.maximum(jnp.dot(hn, w1_ref[:, cols], preferred_element_type=F32), 0.0)
        t = jnp.dot((a * a).astype(BF16), w2_ref[cols, :], preferred_element_type=F32)
        acc = t if acc is None else acc + t
    return h + _rms_scale(acc) * g_post


def _conv_mixer_kernel(x_ref, gpre_ref, wpw1_ref, bpw1_ref, wdw_ref, bdw_ref, lng_ref, lnb_ref,
                       wpw2_ref, bpw2_ref, gpost_ref, o_ref, buf_ref, cv_ref):
    tm, d = x_ref.shape[1], x_ref.shape[2]

    @pl.when(pl.program_id(1) == 0)
    def _zero_history():
        buf_ref[0:HALO, :] = jnp.zeros((HALO, d), F32)

    x = x_ref[0]
    hn = (_rms_scale(x) * gpre_ref[...]).astype(BF16)
    u = jnp.dot(hn, wpw1_ref[...], preferred_element_type=F32) + bpw1_ref[...]
    buf_ref[HALO:HALO + tm, :] = u[:, :d] * jax.nn.sigmoid(u[:, d:])

    first_off = HALO - (CONV_WIDTH - 1)

    def conv_rows(i, carry):
        r0 = pl.multiple_of(i * CONV_ROWS, CONV_ROWS)
        for c in range(d // CONV_LANES):
            lanes = slice(c * CONV_LANES, (c + 1) * CONV_LANES)
            out = jnp.broadcast_to(bdw_ref[:, lanes], (CONV_ROWS, CONV_LANES))
            for r in range(SUBLANES):
                rows = CONV_ROWS + (SUBLANES if r else 0)
                group = None
                for off in range(r, HALO + 1, SUBLANES):
                    if off < first_off:
                        continue
                    k = off - first_off
                    term = wdw_ref[k:k + 1, lanes] * buf_ref[pl.ds(r0 + (off - r), rows), lanes]
                    group = term if group is None else group + term
                out = out + group[r:r + CONV_ROWS]
            cv_ref[pl.ds(r0, CONV_ROWS), lanes] = out
        return carry

    lax.fori_loop(0, tm // CONV_ROWS, conv_rows, 0)
    buf_ref[0:HALO, :] = buf_ref[tm:tm + HALO, :]

    cv = cv_ref[...]
    xc = cv - jnp.mean(cv, axis=-1, keepdims=True)
    y = xc * lax.rsqrt(jnp.mean(xc * xc, axis=-1, keepdims=True) + LN_EPS) * lng_ref[...] + lnb_ref[...]
    y = (y * jax.nn.sigmoid(y)).astype(BF16)
    m = jnp.dot(y, wpw2_ref[...], preferred_element_type=F32) + bpw2_ref[...]
    o_ref[0] = x + _rms_scale(m) * gpost_ref[...]


def _ffn_proj_kernel(h_ref, cs_ref, gpre_ref, w1_ref, w2_ref, gpost_ref,
                     gkv_ref, wdkv_ref, gckv_ref, wuk_ref, wuv_ref,
                     gq_ref, wdq_ref, gcq_ref, wuq_ref,
                     h_out_ref, k_ref, vt_ref, qt_ref):
    tm = h_ref.shape[1]
    h = _ffn(h_ref[0], gpre_ref[...], w1_ref, w2_ref, gpost_ref[...])
    h_out_ref[0] = h

    hs = _rms_scale(h)
    cs = cs_ref[...]
    c_rank = gckv_ref.shape[1]
    lane = lax.broadcasted_iota(jnp.int32, (tm, 2 * ROPE), 1)

    ckv = jnp.dot((hs * gkv_ref[...]).astype(BF16), wdkv_ref[...], preferred_element_type=F32)
    c_kv = (_rms_scale(ckv[:, :c_rank]) * gckv_ref[...]).astype(BF16)
    kr = ckv[:, c_rank:] * cs
    kr = kr + pltpu.roll(kr, ROPE, 1)
    kr = jnp.where(lane < ROPE, kr, 0.0).astype(BF16)
    kn = jnp.dot(c_kv, wuk_ref[...], preferred_element_type=F32)
    v = jnp.dot(c_kv, wuv_ref[...], preferred_element_type=F32)
    for hd in range(N_HEADS):
        k_ref[0, hd, :, 0:NOPE] = kn[:, hd * NOPE:(hd + 1) * NOPE].astype(BF16)
        k_ref[0, hd, :, NOPE:KPAD] = kr
        for blk in range(tm // TK):
            vh = v[blk * TK:(blk + 1) * TK, hd * V_DIM:(hd + 1) * V_DIM]
            vt_ref[0, hd, blk] = vh.T.astype(BF16)

    cq = jnp.dot((hs * gq_ref[...]).astype(BF16), wdq_ref[...], preferred_element_type=F32)
    cq = (_rms_scale(cq) * gcq_ref[...]).astype(BF16)
    q = jnp.dot(cq, wuq_ref[...], preferred_element_type=F32) * Q_SCALE
    for hd in range(N_HEADS):
        qn = q[:, hd * KPAD:hd * KPAD + NOPE]
        qr = q[:, hd * KPAD + NOPE:(hd + 1) * KPAD] * cs
        qr = qr + pltpu.roll(qr, ROPE, 1)
        for blk in range(tm // TQ):
            rows = slice(blk * TQ, (blk + 1) * TQ)
            qt_ref[0, hd, blk, 0:NOPE, :] = qn[rows].T.astype(BF16)
            qt_ref[0, hd, blk, NOPE:KPAD, :] = qr[rows].T.astype(BF16)


def _attn_kernel(qtab_ref, jtab_ref, qt_ref, k_ref, vt_ref, bias_ref, ot_ref,
                 s_ref, mx_ref, p_ref, al_ref, m_ref, l_ref, acc_ref):
    nq = qt_ref.shape[2]
    n_items = qtab_ref.shape[0]
    n_groups = n_items // ATTN_GROUP
    n_diag_groups = nq // ATTN_GROUP

    def qk(t, slot, first):
        qi, j = qtab_ref[t], jtab_ref[t]
        k = k_ref[0, 0, pl.ds(pl.multiple_of(j * TK, TK), TK), :]
        s = jnp.dot(k, qt_ref[0, 0, qi], preferred_element_type=F32)
        if first:
            s = s + bias_ref[...]
        s_ref[slot] = s
        mx_ref[slot] = jnp.max(s, axis=0, keepdims=True)

    def softmax(t, slot, first):
        qi = qtab_ref[t]
        if first:
            m_new = mx_ref[slot]
            p = jnp.exp2(s_ref[slot] - m_new)
            l_ref[qi] = jnp.sum(p, axis=0, keepdims=True)
        else:
            m_old = m_ref[qi]
            m_new = jnp.maximum(m_old, mx_ref[slot])
            alpha = jnp.exp2(m_old - m_new)
            p = jnp.exp2(s_ref[slot] - m_new)
            l_ref[qi] = alpha * l_ref[qi] + jnp.sum(p, axis=0, keepdims=True)
            al_ref[slot] = alpha
        m_ref[qi] = m_new
        p_ref[slot] = p.astype(BF16)

    def pv(t, slot, first):
        qi, j = qtab_ref[t], jtab_ref[t]
        pv_t = jnp.dot(vt_ref[0, 0, j], p_ref[slot], preferred_element_type=F32)
        acc_ref[qi] = pv_t if first else al_ref[slot] * acc_ref[qi] + pv_t

    def body(g, parity, diag_qk, diag_pv, diag_sm):
        stages = ((qk, g, parity, diag_qk), (pv, g - 2, parity, diag_pv), (softmax, g - 1, 1 - parity, diag_sm))
        for stage, group, par, diag in stages:
            if diag is not None:
                for i in range(ATTN_GROUP):
                    stage(group * ATTN_GROUP + i, par * ATTN_GROUP + i, diag)

    def static_body(g):
        flag = lambda grp: (grp < n_diag_groups) if 0 <= grp < n_groups else None
        body(g, g % 2, flag(g), flag(g - 2), flag(g - 1))

    loop_lo = n_diag_groups + 2
    n_pairs = max(n_groups - loop_lo, 0) // 2
    loop_hi = loop_lo + 2 * n_pairs
    for g in range(0, min(loop_lo, n_groups + 2)):
        static_body(g)

    def body_pair(u, carry):
        g = loop_lo + 2 * u
        body(g, loop_lo % 2, False, False, False)
        body(g + 1, (loop_lo + 1) % 2, False, False, False)
        return carry

    if n_pairs:
        lax.fori_loop(0, n_pairs, body_pair, 0)
    for g in range(max(loop_hi, loop_lo), n_groups + 2):
        static_body(g)

    def finish(qi, carry):
        ot_ref[0, 0, qi] = (acc_ref[qi] * (1.0 / l_ref[qi])).astype(BF16)
        return carry

    lax.fori_loop(0, nq, finish, 0)


def _out_ffn_kernel(ot_ref, h_ref, wo_ref, gmix_ref, gpre_ref, w1_ref, w2_ref, gpost_ref, out_ref, o_scr):
    for hd in range(ot_ref.shape[1]):
        for blk in range(ot_ref.shape[2]):
            o_t = ot_ref[0, hd, blk].astype(F32)
            o_scr[blk * TQ:(blk + 1) * TQ, hd * V_DIM:(hd + 1) * V_DIM] = o_t.T.astype(BF16)
    m = jnp.dot(o_scr[...], wo_ref[...], preferred_element_type=F32)
    h = h_ref[0] + _rms_scale(m) * gmix_ref[...]
    out_ref[0] = _ffn(h, gpre_ref[...], w1_ref, w2_ref, gpost_ref[...])


def _const_spec(shape):
    zeros = (0,) * len(shape)
    return pl.BlockSpec(shape, lambda *_: zeros, pipeline_mode=pl.Buffered(1))


def _row(v):
    return v.reshape(1, -1).astype(F32)


def _rot_half_cols(w):
    half = w.shape[-1] // 2
    return jnp.concatenate([-w[..., half:], w[..., :half]], axis=-1)


def _params(n_grid_axes):
    return pltpu.CompilerParams(dimension_semantics=("arbitrary",) * n_grid_axes,
                                vmem_limit_bytes=VMEM_LIMIT)


def kernel(x, mix_pre_g, mix_post_g, ffn_pre_g, ffn_post_g, w_ff1, w_ff2, conv_w_pw1, conv_b_pw1, conv_w_dw, conv_b_dw, conv_ln_g, conv_ln_b, conv_w_pw2, conv_b_pw2, mla_w_dq, mla_q_norm_g, mla_w_uq, mla_w_o, kv_in_g, kv_w_dkv, kv_norm_g, kv_w_kr, kv_w_uk, kv_w_uv):
    b, s, d = x.shape
    assert mix_pre_g.shape[0] == 2 and conv_w_pw1.shape[0] == 1 and mla_w_dq.shape[0] == 1
    assert conv_w_dw.shape[1] == CONV_WIDTH
    tm = min(512, s)
    assert s % tm == 0 and tm % TQ == 0 and TQ == TK and d % CONV_LANES == 0
    n_s = s // tm
    d_ff = w_ff1.shape[2]
    q_rank = mla_w_dq.shape[2]
    c_rank = kv_w_dkv.shape[1]

    tile = pl.BlockSpec((1, tm, d), lambda i, j: (i, j, 0))

    h1 = pl.pallas_call(
        _conv_mixer_kernel,
        grid=(b, n_s),
        in_specs=[tile, _const_spec((1, d)), _const_spec((d, 2 * d)), _const_spec((1, 2 * d)),
                  _const_spec((CONV_WIDTH, d)), _const_spec((1, d)), _const_spec((1, d)),
                  _const_spec((1, d)), _const_spec((d, d)), _const_spec((1, d)), _const_spec((1, d))],
        out_specs=tile,
        out_shape=jax.ShapeDtypeStruct((b, s, d), F32),
        scratch_shapes=[pltpu.VMEM((tm + HALO, d), F32), pltpu.VMEM((tm, d), F32)],
        compiler_params=_params(2),
        name="conv_mixer",
    )(x, _row(mix_pre_g[0]), conv_w_pw1[0].astype(BF16), _row(conv_b_pw1[0]),
      conv_w_dw[0].astype(F32), _row(conv_b_dw[0]), _row(conv_ln_g[0]), _row(conv_ln_b[0]),
      conv_w_pw2[0].astype(BF16), _row(conv_b_pw2[0]), _row(mix_post_g[0]))

    inv = 1.0 / (ROPE_BASE ** (jnp.arange(0, ROPE, 2, dtype=F32) / ROPE))
    ang = jnp.arange(s, dtype=F32)[:, None] * inv[None, :]
    cs = jnp.concatenate([jnp.cos(ang), jnp.cos(ang), jnp.sin(ang), jnp.sin(ang)], axis=-1)

    w_dkv_kr = jnp.concatenate([kv_w_dkv, kv_w_kr, _rot_half_cols(kv_w_kr)], axis=-1).astype(BF16)
    w_uq = mla_w_uq[0].reshape(q_rank, N_HEADS, QK_DIM)
    w_uq = jnp.concatenate([w_uq, _rot_half_cols(w_uq[..., NOPE:])], axis=-1)
    w_uq = w_uq.reshape(q_rank, N_HEADS * KPAD).astype(BF16)

    h2, k_cat, v_t, q_t = pl.pallas_call(
        _ffn_proj_kernel,
        grid=(b, n_s),
        in_specs=[tile, pl.BlockSpec((tm, 2 * ROPE), lambda i, j: (j, 0)),
                  _const_spec((1, d)), _const_spec((d, d_ff)), _const_spec((d_ff, d)), _const_spec((1, d)),
                  _const_spec((1, d)), _const_spec((d, c_rank + 2 * ROPE)), _const_spec((1, c_rank)),
                  _const_spec((c_rank, N_HEADS * NOPE)), _const_spec((c_rank, N_HEADS * V_DIM)),
                  _const_spec((1, d)), _const_spec((d, q_rank)), _const_spec((1, q_rank)),
                  _const_spec((q_rank, N_HEADS * KPAD))],
        out_specs=[tile,
                   pl.BlockSpec((1, N_HEADS, tm, KPAD), lambda i, j: (i, 0, j, 0)),
                   pl.BlockSpec((1, N_HEADS, tm // TK, V_DIM, TK), lambda i, j: (i, 0, j, 0, 0)),
                   pl.BlockSpec((1, N_HEADS, tm // TQ, KPAD, TQ), lambda i, j: (i, 0, j, 0, 0))],
        out_shape=[jax.ShapeDtypeStruct((b, s, d), F32),
                   jax.ShapeDtypeStruct((b, N_HEADS, s, KPAD), BF16),
                   jax.ShapeDtypeStruct((b, N_HEADS, s // TK, V_DIM, TK), BF16),
                   jax.ShapeDtypeStruct((b, N_HEADS, s // TQ, KPAD, TQ), BF16)],
        compiler_params=_params(2),
        name="ffn_proj",
    )(h1, cs, _row(ffn_pre_g[0]), w_ff1[0].astype(BF16), w_ff2[0].astype(BF16), _row(ffn_post_g[0]),
      _row(kv_in_g), w_dkv_kr, _row(kv_norm_g), kv_w_uk.astype(BF16), kv_w_uv.astype(BF16),
      _row(mix_pre_g[1]), mla_w_dq[0].astype(BF16), _row(mla_q_norm_g[0]), w_uq)

    nq = s // TQ
    assert nq % ATTN_GROUP == 0 and (nq * (nq - 1) // 2) % ATTN_GROUP == 0
    n_slots = 2 * ATTN_GROUP
    items = [(qi, qi) for qi in range(nq)] + [(qi, j) for qi in range(nq) for j in range(qi)]
    qtab = jnp.asarray([qi for qi, _ in items], jnp.int32)
    jtab = jnp.asarray([j for _, j in items], jnp.int32)
    key_chunk = lax.broadcasted_iota(jnp.int32, (TK, TQ), 0) // CHUNK
    qry_chunk = lax.broadcasted_iota(jnp.int32, (TK, TQ), 1) // CHUNK
    bias = jnp.where(key_chunk <= qry_chunk, 0.0, -jnp.inf).astype(F32)
    o_t = pl.pallas_call(
        _attn_kernel,
        grid_spec=pltpu.PrefetchScalarGridSpec(
            num_scalar_prefetch=2,
            grid=(b, N_HEADS),
            in_specs=[pl.BlockSpec((1, 1, nq, KPAD, TQ), lambda i, h, *_: (i, h, 0, 0, 0)),
                      pl.BlockSpec((1, 1, s, KPAD), lambda i, h, *_: (i, h, 0, 0)),
                      pl.BlockSpec((1, 1, s // TK, V_DIM, TK), lambda i, h, *_: (i, h, 0, 0, 0)),
                      _const_spec((TK, TQ))],
            out_specs=pl.BlockSpec((1, 1, nq, V_DIM, TQ), lambda i, h, *_: (i, h, 0, 0, 0)),
            scratch_shapes=[pltpu.VMEM((n_slots, TK, TQ), F32), pltpu.VMEM((n_slots, 1, TQ), F32),
                            pltpu.VMEM((n_slots, TK, TQ), BF16), pltpu.VMEM((n_slots, 1, TQ), F32),
                            pltpu.VMEM((nq, 1, TQ), F32), pltpu.VMEM((nq, 1, TQ), F32),
                            pltpu.VMEM((nq, V_DIM, TQ), F32)]),
        out_shape=jax.ShapeDtypeStruct((b, N_HEADS, nq, V_DIM, TQ), BF16),
        compiler_params=_params(2),
        name="attention",
    )(qtab, jtab, q_t, k_cat, v_t, bias)

    return pl.pallas_call(
        _out_ffn_kernel,
        grid=(b, n_s),
        in_specs=[pl.BlockSpec((1, N_HEADS, tm // TQ, V_DIM, TQ), lambda i, j: (i, 0, j, 0, 0)), tile,
                  _const_spec((N_HEADS * V_DIM, d)), _const_spec((1, d)), _const_spec((1, d)),
                  _const_spec((d, d_ff)), _const_spec((d_ff, d)), _const_spec((1, d))],
        out_specs=tile,
        out_shape=jax.ShapeDtypeStruct((b, s, d), F32),
        scratch_shapes=[pltpu.VMEM((tm, N_HEADS * V_DIM), BF16)],
        compiler_params=_params(2),
        name="out_ffn",
    )(o_t, h2, mla_w_o[0].astype(BF16), _row(mix_post_g[1]), _row(ffn_pre_g[1]),
      w_ff1[1].astype(BF16), w_ff2[1].astype(BF16), _row(ffn_post_g[1]))
```

```python
import functools

import jax
import jax.numpy as jnp
from jax import lax
from jax.experimental import pallas as pl
from jax.experimental.pallas import tpu as pltpu

F32 = jnp.float32
BF16 = jnp.bfloat16

RMS_EPS = 1e-6
LN_EPS = 1e-5
ROPE_BASE = 10000.0
CHUNK = 64
CONV_WIDTH = 31
N_HEADS = 8
NOPE = 128
ROPE = 64
V_DIM = 128
QK_DIM = NOPE + ROPE
KPAD = 256
LOG2_E = 1.4426950408889634
Q_SCALE = QK_DIM ** -0.5 * LOG2_E

SUBLANES = 8
HALO = 32
CONV_ROWS = 64
CONV_LANES = 256
FFN_COLS = 1024
TQ = 256
TK = 256
ATTN_GROUP = 4
VMEM_LIMIT = 56 * 1024 * 1024


def _rms_scale(x):
    return x * lax.rsqrt(jnp.mean(x * x, axis=-1, keepdims=True) + RMS_EPS)


def _ffn(h, g_pre, w1_ref, w2_ref, g_post):
    hn = (_rms_scale(h) * g_pre).astype(BF16)
    d_ff = w1_ref.shape[1]
    acc = None
    for c in range(d_ff // FFN_COLS):
        cols = slice(c * FFN_COLS, (c + 1) * FFN_COLS)
        a = jnp.maximum(jnp.dot(hn, w1_ref[:, cols], preferred_element_type=F32), 0.0)
        t = jnp.dot((a * a).astype(BF16), w2_ref[cols, :], preferred_element_type=F32)
        acc = t if acc is None else acc + t
    return h + _rms_scale(acc) * g_post


def _mixer_ffn_kernel(xa_ref, xb_ref, gpre_ref, wpw1_ref, bpw1_ref, wdw_ref, bdw_ref, lng_ref, lnb_ref,
                      wpw2_ref, bpw2_ref, gmix_ref, gffn_ref, w1_ref, w2_ref, gpost_ref,
                      o_ref, buf_ref, cv_ref, *, tiles_per_seq):
    g = pl.program_id(0)
    tm, d = xa_ref.shape

    @pl.when(g == 0)
    def _no_previous_tile():
        cv_ref[...] = jnp.zeros(cv_ref.shape, F32)

    @pl.when(lax.rem(g, tiles_per_seq) == 0)
    def _zero_history():
        buf_ref[0:HALO, :] = jnp.zeros((HALO, d), F32)

    hn = (_rms_scale(xa_ref[...]) * gpre_ref[...]).astype(BF16)
    u = jnp.dot(hn, wpw1_ref[...], preferred_element_type=F32) + bpw1_ref[...]
    buf_ref[HALO:HALO + tm, :] = u[:, :d] * jax.nn.sigmoid(u[:, d:])

    cv = cv_ref[...]
    xc = cv - jnp.mean(cv, axis=-1, keepdims=True)
    y = xc * lax.rsqrt(jnp.mean(xc * xc, axis=-1, keepdims=True) + LN_EPS) * lng_ref[...] + lnb_ref[...]
    y = (y * jax.nn.sigmoid(y)).astype(BF16)
    m = jnp.dot(y, wpw2_ref[...], preferred_element_type=F32) + bpw2_ref[...]
    h1 = xb_ref[...] + _rms_scale(m) * gmix_ref[...]
    hn1 = (_rms_scale(h1) * gffn_ref[...]).astype(BF16)

    first_off = HALO - (CONV_WIDTH - 1)

    def zero_after(v):
        bits = pltpu.bitcast(v, jnp.int32)
        half = jnp.full(bits.shape, 16, jnp.int32)
        return lax.shift_right_logical(lax.shift_right_logical(bits, half), half).astype(F32)

    def conv_unit(r0, c, pace):
        lanes = slice(c * CONV_LANES, (c + 1) * CONV_LANES)
        out = bdw_ref[:, lanes] + zero_after(pace)
        out = jnp.broadcast_to(out[0:1], (CONV_ROWS, CONV_LANES))
        for r in range(SUBLANES):
            rows = CONV_ROWS + (SUBLANES if r else 0)
            group = None
            for off in range(r, HALO + 1, SUBLANES):
                if off < first_off:
                    continue
                k = off - first_off
                start = r0 + off - r
                term = wdw_ref[k:k + 1, lanes] * buf_ref[start:start + rows, lanes]
                group = term if group is None else group + term
            out = out + group[r:r + CONV_ROWS]
        cv_ref[r0:r0 + CONV_ROWS, lanes] = out

    n_slabs = w1_ref.shape[1] // FFN_COLS
    n_lane_chunks = d // CONV_LANES
    units = [(r0, c) for r0 in range(0, tm, CONV_ROWS) for c in range(n_lane_chunks)]
    per_slab = -(-len(units) // n_slabs)
    acc = None
    for s in range(n_slabs):
        cols = slice(s * FFN_COLS, (s + 1) * FFN_COLS)
        a = jnp.maximum(jnp.dot(hn1, w1_ref[:, cols], preferred_element_type=F32), 0.0)
        t = jnp.dot((a * a).astype(BF16), w2_ref[cols, :], preferred_element_type=F32)
        acc = t if acc is None else acc + t
        slab_units = units[s * per_slab:(s + 1) * per_slab]
        for i, (r0, c) in enumerate(slab_units):
            src = a if 2 * i < len(slab_units) else t
            row = ((2 * i) % len(slab_units)) * (tm // len(slab_units))
            conv_unit(r0, c, src[row:row + SUBLANES, 0:CONV_LANES])
    o_ref[...] = h1 + _rms_scale(acc) * gpost_ref[...]
    buf_ref[0:HALO, :] = buf_ref[tm:tm + HALO, :]


def _proj_kernel(h_ref, cs_ref, gkv_ref, wdkv_ref, gckv_ref, wuk_ref, wuv_ref,
                 gq_ref, wdq_ref, gcq_ref, wuq_ref, k_ref, vt_ref, qt_ref):
    tm = h_ref.shape[1]
    hs = _rms_scale(h_ref[0])
    cs = cs_ref[...]
    c_rank = gckv_ref.shape[1]
    lane = lax.broadcasted_iota(jnp.int32, (tm, 2 * ROPE), 1)

    ckv = jnp.dot((hs * gkv_ref[...]).astype(BF16), wdkv_ref[...], preferred_element_type=F32)
    c_kv = (_rms_scale(ckv[:, :c_rank]) * gckv_ref[...]).astype(BF16)
    kr = ckv[:, c_rank:] * cs
    kr = kr + pltpu.roll(kr, ROPE, 1)
    kr = jnp.where(lane < ROPE, kr, 0.0).astype(BF16)
    kn = jnp.dot(c_kv, wuk_ref[...], preferred_element_type=F32)
    v = jnp.dot(c_kv, wuv_ref[...], preferred_element_type=F32)
    for hd in range(N_HEADS):
        k_ref[0, hd, :, 0:NOPE] = kn[:, hd * NOPE:(hd + 1) * NOPE].astype(BF16)
        k_ref[0, hd, :, NOPE:KPAD] = kr
        for blk in range(tm // TK):
            vh = v[blk * TK:(blk + 1) * TK, hd * V_DIM:(hd + 1) * V_DIM]
            vt_ref[0, hd, blk] = vh.T.astype(BF16)

    cq = jnp.dot((hs * gq_ref[...]).astype(BF16), wdq_ref[...], preferred_element_type=F32)
    cq = (_rms_scale(cq) * gcq_ref[...]).astype(BF16)
    q = jnp.dot(cq, wuq_ref[...], preferred_element_type=F32) * Q_SCALE
    for hd in range(N_HEADS):
        qn = q[:, hd * KPAD:hd * KPAD + NOPE]
        qr = q[:, hd * KPAD + NOPE:(hd + 1) * KPAD] * cs
        qr = qr + pltpu.roll(qr, ROPE, 1)
        for blk in range(tm // TQ):
            rows = slice(blk * TQ, (blk + 1) * TQ)
            qt_ref[0, hd, blk, 0:NOPE, :] = qn[rows].T.astype(BF16)
            qt_ref[0, hd, blk, NOPE:KPAD, :] = qr[rows].T.astype(BF16)


def _attn_kernel(qtab_ref, jtab_ref, qt_ref, k_ref, vt_ref, bias_ref, ot_ref,
                 s_ref, mx_ref, p_ref, al_ref, m_ref, l_ref, acc_ref):
    nq = qt_ref.shape[2]
    n_items = qtab_ref.shape[0]
    n_groups = n_items // ATTN_GROUP
    n_diag_groups = nq // ATTN_GROUP

    def qk(t, slot, first):
        qi, j = qtab_ref[t], jtab_ref[t]
        k = k_ref[0, 0, pl.ds(pl.multiple_of(j * TK, TK), TK), :]
        s = jnp.dot(k, qt_ref[0, 0, qi], preferred_element_type=F32)
        if first:
            s = s + bias_ref[...]
        s_ref[slot] = s
        mx_ref[slot] = jnp.max(s, axis=0, keepdims=True)

    def softmax(t, slot, first):
        qi = qtab_ref[t]
        if first:
            m_new = mx_ref[slot]
            p = jnp.exp2(s_ref[slot] - m_new)
            l_ref[qi] = jnp.sum(p, axis=0, keepdims=True)
        else:
            m_old = m_ref[qi]
            m_new = jnp.maximum(m_old, mx_ref[slot])
            alpha = jnp.exp2(m_old - m_new)
            p = jnp.exp2(s_ref[slot] - m_new)
            l_ref[qi] = alpha * l_ref[qi] + jnp.sum(p, axis=0, keepdims=True)
            al_ref[slot] = alpha
        m_ref[qi] = m_new
        p_ref[slot] = p.astype(BF16)

    def pv(t, slot, first):
        qi, j = qtab_ref[t], jtab_ref[t]
        pv_t = jnp.dot(vt_ref[0, 0, j], p_ref[slot], preferred_element_type=F32)
        acc_ref[qi] = pv_t if first else al_ref[slot] * acc_ref[qi] + pv_t

    def body(g, parity, diag_qk, diag_pv, diag_sm):
        stages = ((qk, g, parity, diag_qk), (pv, g - 2, parity, diag_pv), (softmax, g - 1, 1 - parity, diag_sm))
        for stage, group, par, diag in stages:
            if diag is not None:
                for i in range(ATTN_GROUP):
                    stage(group * ATTN_GROUP + i, par * ATTN_GROUP + i, diag)

    def static_body(g):
        flag = lambda grp: (grp < n_diag_groups) if 0 <= grp < n_groups else None
        body(g, g % 2, flag(g), flag(g - 2), flag(g - 1))

    loop_lo = n_diag_groups + 2
    n_pairs = max(n_groups - loop_lo, 0) // 2
    loop_hi = loop_lo + 2 * n_pairs
    for g in range(0, min(loop_lo, n_groups + 2)):
        static_body(g)

    def body_pair(u, carry):
        g = loop_lo + 2 * u
        body(g, loop_lo % 2, False, False, False)
        body(g + 1, (loop_lo + 1) % 2, False, False, False)
        return carry

    if n_pairs:
        lax.fori_loop(0, n_pairs, body_pair, 0)
    for g in range(max(loop_hi, loop_lo), n_groups + 2):
        static_body(g)

    def finish(qi, carry):
        ot_ref[0, 0, qi] = (acc_ref[qi] * (1.0 / l_ref[qi])).astype(BF16)
        return carry

    lax.fori_loop(0, nq, finish, 0)


def _out_ffn_kernel(ot_ref, h_ref, wo_ref, gmix_ref, gpre_ref, w1_ref, w2_ref, gpost_ref, out_ref, o_scr):
    for hd in range(ot_ref.shape[1]):
        for blk in range(ot_ref.shape[2]):
            o_t = ot_ref[0, hd, blk].astype(F32)
            o_scr[blk * TQ:(blk + 1) * TQ, hd * V_DIM:(hd + 1) * V_DIM] = o_t.T.astype(BF16)
    m = jnp.dot(o_scr[...], wo_ref[...], preferred_element_type=F32)
    h = h_ref[0] + _rms_scale(m) * gmix_ref[...]
    out_ref[0] = _ffn(h, gpre_ref[...], w1_ref, w2_ref, gpost_ref[...])


def _const_spec(shape):
    zeros = (0,) * len(shape)
    return pl.BlockSpec(shape, lambda *_: zeros, pipeline_mode=pl.Buffered(1))


def _row(v):
    return v.reshape(1, -1).astype(F32)


def _rot_half_cols(w):
    half = w.shape[-1] // 2
    return jnp.concatenate([-w[..., half:], w[..., :half]], axis=-1)


def _params(n_grid_axes, flags=None):
    return pltpu.CompilerParams(dimension_semantics=("arbitrary",) * n_grid_axes,
                                vmem_limit_bytes=VMEM_LIMIT, flags=flags)


def kernel(x, mix_pre_g, mix_post_g, ffn_pre_g, ffn_post_g, w_ff1, w_ff2, conv_w_pw1, conv_b_pw1, conv_w_dw, conv_b_dw, conv_ln_g, conv_ln_b, conv_w_pw2, conv_b_pw2, mla_w_dq, mla_q_norm_g, mla_w_uq, mla_w_o, kv_in_g, kv_w_dkv, kv_norm_g, kv_w_kr, kv_w_uk, kv_w_uv):
    b, s, d = x.shape
    assert mix_pre_g.shape[0] == 2 and conv_w_pw1.shape[0] == 1 and mla_w_dq.shape[0] == 1
    assert conv_w_dw.shape[1] == CONV_WIDTH
    tm = min(512, s)
    assert s % tm == 0 and tm % TQ == 0 and TQ == TK and d % CONV_LANES == 0
    n_s = s // tm
    d_ff = w_ff1.shape[2]
    q_rank = mla_w_dq.shape[2]
    c_rank = kv_w_dkv.shape[1]

    tile = pl.BlockSpec((1, tm, d), lambda i, j: (i, j, 0))

    n_t = b * n_s
    x2 = x.reshape(n_t * tm, d)
    h2 = pl.pallas_call(
        functools.partial(_mixer_ffn_kernel, tiles_per_seq=n_s),
        grid=(n_t + 1,),
        in_specs=[pl.BlockSpec((tm, d), lambda g: (jnp.minimum(g, n_t - 1), 0)),
                  pl.BlockSpec((tm, d), lambda g: (jnp.maximum(g - 1, 0), 0)),
                  _const_spec((1, d)), _const_spec((d, 2 * d)), _const_spec((1, 2 * d)),
                  _const_spec((CONV_WIDTH, d)), _const_spec((1, d)), _const_spec((1, d)),
                  _const_spec((1, d)), _const_spec((d, d)), _const_spec((1, d)), _const_spec((1, d)),
                  _const_spec((1, d)), _const_spec((d, d_ff)), _const_spec((d_ff, d)), _const_spec((1, d))],
        out_specs=pl.BlockSpec((tm, d), lambda g: (jnp.maximum(g - 1, 0), 0)),
        out_shape=jax.ShapeDtypeStruct((n_t * tm, d), F32),
        scratch_shapes=[pltpu.VMEM((tm + HALO, d), F32), pltpu.VMEM((tm, d), F32)],
        compiler_params=_params(1),
        name="mixer_ffn",
    )(x2, x2, _row(mix_pre_g[0]), conv_w_pw1[0].astype(BF16), _row(conv_b_pw1[0]),
      conv_w_dw[0].astype(F32), _row(conv_b_dw[0]), _row(conv_ln_g[0]), _row(conv_ln_b[0]),
      conv_w_pw2[0].astype(BF16), _row(conv_b_pw2[0]), _row(mix_post_g[0]),
      _row(ffn_pre_g[0]), w_ff1[0].astype(BF16), w_ff2[0].astype(BF16), _row(ffn_post_g[0]))
    h2 = h2.reshape(b, s, d)

    inv = 1.0 / (ROPE_BASE ** (jnp.arange(0, ROPE, 2, dtype=F32) / ROPE))
    ang = jnp.arange(s, dtype=F32)[:, None] * inv[None, :]
    cs = jnp.concatenate([jnp.cos(ang), jnp.cos(ang), jnp.sin(ang), jnp.sin(ang)], axis=-1)

    w_dkv_kr = jnp.concatenate([kv_w_dkv, kv_w_kr, _rot_half_cols(kv_w_kr)], axis=-1).astype(BF16)
    w_uq = mla_w_uq[0].reshape(q_rank, N_HEADS, QK_DIM)
    w_uq = jnp.concatenate([w_uq, _rot_half_cols(w_uq[..., NOPE:])], axis=-1)
    w_uq = w_uq.reshape(q_rank, N_HEADS * KPAD).astype(BF16)

    k_cat, v_t, q_t = pl.pallas_call(
        _proj_kernel,
        grid=(b, n_s),
        in_specs=[tile, pl.BlockSpec((tm, 2 * ROPE), lambda i, j: (j, 0)),
                  _const_spec((1, d)), _const_spec((d, c_rank + 2 * ROPE)), _const_spec((1, c_rank)),
                  _const_spec((c_rank, N_HEADS * NOPE)), _const_spec((c_rank, N_HEADS * V_DIM)),
                  _const_spec((1, d)), _const_spec((d, q_rank)), _const_spec((1, q_rank)),
                  _const_spec((q_rank, N_HEADS * KPAD))],
        out_specs=[pl.BlockSpec((1, N_HEADS, tm, KPAD), lambda i, j: (i, 0, j, 0)),
                   pl.BlockSpec((1, N_HEADS, tm // TK, V_DIM, TK), lambda i, j: (i, 0, j, 0, 0)),
                   pl.BlockSpec((1, N_HEADS, tm // TQ, KPAD, TQ), lambda i, j: (i, 0, j, 0, 0))],
        out_shape=[jax.ShapeDtypeStruct((b, N_HEADS, s, KPAD), BF16),
                   jax.ShapeDtypeStruct((b, N_HEADS, s // TK, V_DIM, TK), BF16),
                   jax.ShapeDtypeStruct((b, N_HEADS, s // TQ, KPAD, TQ), BF16)],
        compiler_params=_params(2),
        name="proj",
    )(h2, cs, _row(kv_in_g), w_dkv_kr, _row(kv_norm_g), kv_w_uk.astype(BF16), kv_w_uv.astype(BF16),
      _row(mix_pre_g[1]), mla_w_dq[0].astype(BF16), _row(mla_q_norm_g[0]), w_uq)

    nq = s // TQ
    assert nq % ATTN_GROUP == 0 and (nq * (nq - 1) // 2) % ATTN_GROUP == 0
    n_slots = 2 * ATTN_GROUP
    items = [(qi, qi) for qi in range(nq)] + [(qi, j) for qi in range(nq) for j in range(qi)]
    qtab = jnp.asarray([qi for qi, _ in items], jnp.int32)
    jtab = jnp.asarray([j for _, j in items], jnp.int32)
    key_chunk = lax.broadcasted_iota(jnp.int32, (TK, TQ), 0) // CHUNK
    qry_chunk = lax.broadcasted_iota(jnp.int32, (TK, TQ), 1) // CHUNK
    bias = jnp.where(key_chunk <= qry_chunk, 0.0, -jnp.inf).astype(F32)
    o_t = pl.pallas_call(
        _attn_kernel,
        grid_spec=pltpu.PrefetchScalarGridSpec(
            num_scalar_prefetch=2,
            grid=(b, N_HEADS),
            in_specs=[pl.BlockSpec((1, 1, nq, KPAD, TQ), lambda i, h, *_: (i, h, 0, 0, 0)),
                      pl.BlockSpec((1, 1, s, KPAD), lambda i, h, *_: (i, h, 0, 0)),
                      pl.BlockSpec((1, 1, s // TK, V_DIM, TK), lambda i, h, *_: (i, h, 0, 0, 0)),
                      _const_spec((TK, TQ))],
            out_specs=pl.BlockSpec((1, 1, nq, V_DIM, TQ), lambda i, h, *_: (i, h, 0, 0, 0)),
            scratch_shapes=[pltpu.VMEM((n_slots, TK, TQ), F32), pltpu.VMEM((n_slots, 1, TQ), F32),
                            pltpu.VMEM((n_slots, TK, TQ), BF16), pltpu.VMEM((n_slots, 1, TQ), F32),
                            pltpu.VMEM((nq, 1, TQ), F32), pltpu.VMEM((nq, 1, TQ), F32),
                            pltpu.VMEM((nq, V_DIM, TQ), F32)]),
        out_shape=jax.ShapeDtypeStruct((b, N_HEADS, nq, V_DIM, TQ), BF16),
        compiler_params=_params(2),
        name="attention",
    )(qtab, jtab, q_t, k_cat, v_t, bias)

    return pl.pallas_call(
        _out_ffn_kernel,
        grid=(b, n_s),
        in_specs=[pl.BlockSpec((1, N_HEADS, tm // TQ, V_DIM, TQ), lambda i, j: (i, 0, j, 0, 0)), tile,
                  _const_spec((N_HEADS * V_DIM, d)), _const_spec((1, d)), _const_spec((1, d)),
                  _const_spec((d, d_ff)), _const_spec((d_ff, d)), _const_spec((1, d))],
        out_specs=tile,
        out_shape=jax.ShapeDtypeStruct((b, s, d), F32),
        scratch_shapes=[pltpu.VMEM((tm, N_HEADS * V_DIM), BF16)],
        compiler_params=_params(2),
        name="out_ffn",
    )(o_t, h2, mla_w_o[0].astype(BF16), _row(mix_post_g[1]), _row(ffn_pre_g[1]),
      w_ff1[1].astype(BF16), w_ff2[1].astype(BF16), _row(ffn_post_g[1]))
```

```python
import functools

import jax
import jax.numpy as jnp
from jax import lax
from jax.experimental import pallas as pl
from jax.experimental.pallas import tpu as pltpu

F32 = jnp.float32
BF16 = jnp.bfloat16

RMS_EPS = 1e-6
LN_EPS = 1e-5
ROPE_BASE = 10000.0
CHUNK = 64
CONV_WIDTH = 31
N_HEADS = 8
NOPE = 128
ROPE = 64
V_DIM = 128
QK_DIM = NOPE + ROPE
KPAD = 256
LOG2_E = 1.4426950408889634
Q_SCALE = QK_DIM ** -0.5 * LOG2_E

SUBLANES = 8
HALO = 32
CONV_ROWS = 64
CONV_LANES = 256
FFN_COLS = 1024
TQ = 256
TK = 256
ATTN_GROUP = 4
ATTN_UNROLL = 4
VMEM_LIMIT = 56 * 1024 * 1024


def _rms_scale(x):
    return x * lax.rsqrt(jnp.mean(x * x, axis=-1, keepdims=True) + RMS_EPS)


def _ffn(h, g_pre, w1_ref, w2_ref, g_post):
    hn = (_rms_scale(h) * g_pre).astype(BF16)
    d_ff = w1_ref.shape[1]
    acc = None
    for c in range(d_ff // FFN_COLS):
        cols = slice(c * FFN_COLS, (c + 1) * FFN_COLS)
        a = jnp.maximum(jnp.dot(hn, w1_ref[:, cols], preferred_element_type=F32), 0.0)
        t = jnp.dot((a * a).astype(BF16), w2_ref[cols, :], preferred_element_type=F32)
        acc = t if acc is None else acc + t
    return h + _rms_scale(acc) * g_post


def _mixer_ffn_kernel(xa_ref, xb_ref, gpre_ref, wpw1_ref, bpw1_ref, wdw_ref, bdw_ref, lng_ref, lnb_ref,
                      wpw2_ref, bpw2_ref, gmix_ref, gffn_ref, w1_ref, w2_ref, gpost_ref,
                      o_ref, buf_ref, cv_ref, *, tiles_per_seq):
    g = pl.program_id(0)
    tm, d = xa_ref.shape

    @pl.when(g == 0)
    def _no_previous_tile():
        cv_ref[...] = jnp.zeros(cv_ref.shape, F32)

    @pl.when(lax.rem(g, tiles_per_seq) == 0)
    def _zero_history():
        buf_ref[0:HALO, :] = jnp.zeros((HALO, d), F32)

    hn = (_rms_scale(xa_ref[...]) * gpre_ref[...]).astype(BF16)
    u = jnp.dot(hn, wpw1_ref[...], preferred_element_type=F32) + bpw1_ref[...]
    buf_ref[HALO:HALO + tm, :] = u[:, :d] * jax.nn.sigmoid(u[:, d:])

    cv = cv_ref[...]
    xc = cv - jnp.mean(cv, axis=-1, keepdims=True)
    y = xc * lax.rsqrt(jnp.mean(xc * xc, axis=-1, keepdims=True) + LN_EPS) * lng_ref[...] + lnb_ref[...]
    y = (y * jax.nn.sigmoid(y)).astype(BF16)
    m = jnp.dot(y, wpw2_ref[...], preferred_element_type=F32) + bpw2_ref[...]
    h1 = xb_ref[...] + _rms_scale(m) * gmix_ref[...]
    hn1 = (_rms_scale(h1) * gffn_ref[...]).astype(BF16)

    first_off = HALO - (CONV_WIDTH - 1)

    def zero_after(v):
        bits = pltpu.bitcast(v, jnp.int32)
        half = jnp.full(bits.shape, 16, jnp.int32)
        return lax.shift_right_logical(lax.shift_right_logical(bits, half), half).astype(F32)

    def conv_unit(r0, c, pace):
        lanes = slice(c * CONV_LANES, (c + 1) * CONV_LANES)
        out = bdw_ref[:, lanes] + zero_after(pace)
        out = jnp.broadcast_to(out[0:1], (CONV_ROWS, CONV_LANES))
        for r in range(SUBLANES):
            rows = CONV_ROWS + (SUBLANES if r else 0)
            group = None
            for off in range(r, HALO + 1, SUBLANES):
                if off < first_off:
                    continue
                k = off - first_off
                start = r0 + off - r
                term = wdw_ref[k:k + 1, lanes] * buf_ref[start:start + rows, lanes]
                group = term if group is None else group + term
            out = out + group[r:r + CONV_ROWS]
        cv_ref[r0:r0 + CONV_ROWS, lanes] = out

    n_slabs = w1_ref.shape[1] // FFN_COLS
    n_lane_chunks = d // CONV_LANES
    units = [(r0, c) for r0 in range(0, tm, CONV_ROWS) for c in range(n_lane_chunks)]

    per_slab = -(-len(units) // n_slabs)
    acc = None
    for s in range(n_slabs):
        cols = slice(s * FFN_COLS, (s + 1) * FFN_COLS)
        a = jnp.maximum(jnp.dot(hn1, w1_ref[:, cols], preferred_element_type=F32), 0.0)
        t = jnp.dot((a * a).astype(BF16), w2_ref[cols, :], preferred_element_type=F32)
        acc = t if acc is None else acc + t
        slab_units = units[s * per_slab:(s + 1) * per_slab]
        for i, (r0, c) in enumerate(slab_units):
            src = a if 2 * i < len(slab_units) else t
            row = ((2 * i) % len(slab_units)) * (tm // len(slab_units))
            conv_unit(r0, c, src[row:row + SUBLANES, 0:CONV_LANES])
    o_ref[...] = h1 + _rms_scale(acc) * gpost_ref[...]
    buf_ref[0:HALO, :] = buf_ref[tm:tm + HALO, :]


def _proj_kernel(h_ref, cs_ref, gkv_ref, wdkv_ref, gckv_ref, wuk_ref, wuv_ref,
                 gq_ref, wdq_ref, gcq_ref, wuq_ref, k_ref, vt_ref, qt_ref):
    tm = h_ref.shape[1]
    hs = _rms_scale(h_ref[0])
    cs = cs_ref[...]
    c_rank = gckv_ref.shape[1]
    lane = lax.broadcasted_iota(jnp.int32, (tm, 2 * ROPE), 1)

    ckv = jnp.dot((hs * gkv_ref[...]).astype(BF16), wdkv_ref[...], preferred_element_type=F32)
    c_kv = (_rms_scale(ckv[:, :c_rank]) * gckv_ref[...]).astype(BF16)
    kr = ckv[:, c_rank:] * cs
    kr = kr + pltpu.roll(kr, ROPE, 1)
    kr = jnp.where(lane < ROPE, kr, 0.0).astype(BF16)
    kn = jnp.dot(c_kv, wuk_ref[...], preferred_element_type=F32)
    v = jnp.dot(c_kv, wuv_ref[...], preferred_element_type=F32)
    for hd in range(N_HEADS):
        k_ref[0, hd, :, 0:NOPE] = kn[:, hd * NOPE:(hd + 1) * NOPE].astype(BF16)
        k_ref[0, hd, :, NOPE:KPAD] = kr
        for blk in range(tm // TK):
            vh = v[blk * TK:(blk + 1) * TK, hd * V_DIM:(hd + 1) * V_DIM]
            vt_ref[0, hd, blk] = vh.T.astype(BF16)

    cq = jnp.dot((hs * gq_ref[...]).astype(BF16), wdq_ref[...], preferred_element_type=F32)
    cq = (_rms_scale(cq) * gcq_ref[...]).astype(BF16)
    q = jnp.dot(cq, wuq_ref[...], preferred_element_type=F32) * Q_SCALE
    for hd in range(N_HEADS):
        qn = q[:, hd * KPAD:hd * KPAD + NOPE]
        qr = q[:, hd * KPAD + NOPE:(hd + 1) * KPAD] * cs
        qr = qr + pltpu.roll(qr, ROPE, 1)
        for blk in range(tm // TQ):
            rows = slice(blk * TQ, (blk + 1) * TQ)
            qt_ref[0, hd, blk, 0:NOPE, :] = qn[rows].T.astype(BF16)
            qt_ref[0, hd, blk, NOPE:KPAD, :] = qr[rows].T.astype(BF16)


def _attn_kernel(qtab_ref, jtab_ref, qt_ref, k_ref, vt_ref, bias_ref, ot_ref,
                 s_ref, mx_ref, p_ref, al_ref, m_ref, l_ref, acc_ref):
    nq = qt_ref.shape[2]
    n_items = qtab_ref.shape[0]
    n_groups = n_items // ATTN_GROUP
    n_diag_groups = nq // ATTN_GROUP

    def qk(t, slot, first):
        qi, j = qtab_ref[t], jtab_ref[t]
        k = k_ref[0, 0, pl.ds(pl.multiple_of(j * TK, TK), TK), :]
        s = jnp.dot(k, qt_ref[0, 0, qi], preferred_element_type=F32)
        if first:
            s = s + bias_ref[...]
        s_ref[slot] = s
        mx_ref[slot] = jnp.max(s, axis=0, keepdims=True)

    def softmax(t, slot, first):
        qi = qtab_ref[t]
        if first:
            m_new = mx_ref[slot]
            p = jnp.exp2(s_ref[slot] - m_new)
            l_ref[qi] = jnp.sum(p, axis=0, keepdims=True)
        else:
            m_old = m_ref[qi]
            m_new = jnp.maximum(m_old, mx_ref[slot])
            alpha = jnp.exp2(m_old - m_new)
            p = jnp.exp2(s_ref[slot] - m_new)
            l_ref[qi] = alpha * l_ref[qi] + jnp.sum(p, axis=0, keepdims=True)
            al_ref[slot] = alpha
        m_ref[qi] = m_new
        p_ref[slot] = p.astype(BF16)

    def pv(t, slot, first):
        qi, j = qtab_ref[t], jtab_ref[t]
        pv_t = jnp.dot(vt_ref[0, 0, j], p_ref[slot], preferred_element_type=F32)
        acc_ref[qi] = pv_t if first else al_ref[slot] * acc_ref[qi] + pv_t

    def body(g, parity, diag_qk, diag_pv, diag_sm):
        stages = ((qk, g, parity, diag_qk), (pv, g - 2, parity, diag_pv), (softmax, g - 1, 1 - parity, diag_sm))
        for stage, group, par, diag in stages:
            if diag is not None:
                for i in range(ATTN_GROUP):
                    stage(group * ATTN_GROUP + i, par * ATTN_GROUP + i, diag)

    def static_body(g):
        flag = lambda grp: (grp < n_diag_groups) if 0 <= grp < n_groups else None
        body(g, g % 2, flag(g), flag(g - 2), flag(g - 1))

    loop_lo = n_diag_groups + 2
    n_iters = max(n_groups - loop_lo, 0) // ATTN_UNROLL
    loop_hi = loop_lo + ATTN_UNROLL * n_iters
    for g in range(0, min(loop_lo, n_groups + 2)):
        static_body(g)

    def body_run(u, carry):
        for i in range(ATTN_UNROLL):
            body(loop_lo + ATTN_UNROLL * u + i, (loop_lo + i) % 2, False, False, False)
        return carry

    if n_iters:
        lax.fori_loop(0, n_iters, body_run, 0)
    for g in range(max(loop_hi, loop_lo), n_groups + 2):
        static_body(g)

    def finish(qi, carry):
        ot_ref[0, 0, qi] = (acc_ref[qi] * (1.0 / l_ref[qi])).astype(BF16)
        return carry

    lax.fori_loop(0, nq, finish, 0)


def _out_ffn_kernel(ot_ref, h_ref, wo_ref, gmix_ref, gpre_ref, w1_ref, w2_ref, gpost_ref, out_ref, o_scr):
    for hd in range(ot_ref.shape[1]):
        for blk in range(ot_ref.shape[2]):
            o_t = ot_ref[0, hd, blk].astype(F32)
            o_scr[blk * TQ:(blk + 1) * TQ, hd * V_DIM:(hd + 1) * V_DIM] = o_t.T.astype(BF16)
    m = jnp.dot(o_scr[...], wo_ref[...], preferred_element_type=F32)
    h = h_ref[0] + _rms_scale(m) * gmix_ref[...]
    out_ref[0] = _ffn(h, gpre_ref[...], w1_ref, w2_ref, gpost_ref[...])


def _const_spec(shape):
    zeros = (0,) * len(shape)
    return pl.BlockSpec(shape, lambda *_: zeros, pipeline_mode=pl.Buffered(1))


def _row(v):
    return v.reshape(1, -1).astype(F32)


def _rot_half_cols(w):
    half = w.shape[-1] // 2
    return jnp.concatenate([-w[..., half:], w[..., :half]], axis=-1)


def _params(n_grid_axes, flags=None):
    return pltpu.CompilerParams(dimension_semantics=("arbitrary",) * n_grid_axes,
                                vmem_limit_bytes=VMEM_LIMIT, flags=flags)


def kernel(x, mix_pre_g, mix_post_g, ffn_pre_g, ffn_post_g, w_ff1, w_ff2, conv_w_pw1, conv_b_pw1, conv_w_dw, conv_b_dw, conv_ln_g, conv_ln_b, conv_w_pw2, conv_b_pw2, mla_w_dq, mla_q_norm_g, mla_w_uq, mla_w_o, kv_in_g, kv_w_dkv, kv_norm_g, kv_w_kr, kv_w_uk, kv_w_uv):
    b, s, d = x.shape
    assert mix_pre_g.shape[0] == 2 and conv_w_pw1.shape[0] == 1 and mla_w_dq.shape[0] == 1
    assert conv_w_dw.shape[1] == CONV_WIDTH
    tm = min(512, s)
    assert s % tm == 0 and tm % TQ == 0 and TQ == TK and d % CONV_LANES == 0
    n_s = s // tm
    d_ff = w_ff1.shape[2]
    q_rank = mla_w_dq.shape[2]
    c_rank = kv_w_dkv.shape[1]

    tile = pl.BlockSpec((1, tm, d), lambda i, j: (i, j, 0))

    n_t = b * n_s
    x2 = x.reshape(n_t * tm, d)
    h2 = pl.pallas_call(
        functools.partial(_mixer_ffn_kernel, tiles_per_seq=n_s),
        grid=(n_t + 1,),
        in_specs=[pl.BlockSpec((tm, d), lambda g: (jnp.minimum(g, n_t - 1), 0)),
                  pl.BlockSpec((tm, d), lambda g: (jnp.maximum(g - 1, 0), 0)),
                  _const_spec((1, d)), _const_spec((d, 2 * d)), _const_spec((1, 2 * d)),
                  _const_spec((CONV_WIDTH, d)), _const_spec((1, d)), _const_spec((1, d)),
                  _const_spec((1, d)), _const_spec((d, d)), _const_spec((1, d)), _const_spec((1, d)),
                  _const_spec((1, d)), _const_spec((d, d_ff)), _const_spec((d_ff, d)), _const_spec((1, d))],
        out_specs=pl.BlockSpec((tm, d), lambda g: (jnp.maximum(g - 1, 0), 0)),
        out_shape=jax.ShapeDtypeStruct((n_t * tm, d), F32),
        scratch_shapes=[pltpu.VMEM((tm + HALO, d), F32), pltpu.VMEM((tm, d), F32)],
        compiler_params=_params(1),
        name="mixer_ffn",
    )(x2, x2, _row(mix_pre_g[0]), conv_w_pw1[0].astype(BF16), _row(conv_b_pw1[0]),
      conv_w_dw[0].astype(F32), _row(conv_b_dw[0]), _row(conv_ln_g[0]), _row(conv_ln_b[0]),
      conv_w_pw2[0].astype(BF16), _row(conv_b_pw2[0]), _row(mix_post_g[0]),
      _row(ffn_pre_g[0]), w_ff1[0].astype(BF16), w_ff2[0].astype(BF16), _row(ffn_post_g[0]))
    h2 = h2.reshape(b, s, d)

    inv = 1.0 / (ROPE_BASE ** (jnp.arange(0, ROPE, 2, dtype=F32) / ROPE))
    ang = jnp.arange(s, dtype=F32)[:, None] * inv[None, :]
    cs = jnp.concatenate([jnp.cos(ang), jnp.cos(ang), jnp.sin(ang), jnp.sin(ang)], axis=-1)

    w_dkv_kr = jnp.concatenate([kv_w_dkv, kv_w_kr, _rot_half_cols(kv_w_kr)], axis=-1).astype(BF16)
    w_uq = mla_w_uq[0].reshape(q_rank, N_HEADS, QK_DIM)
    w_uq = jnp.concatenate([w_uq, _rot_half_cols(w_uq[..., NOPE:])], axis=-1)
    w_uq = w_uq.reshape(q_rank, N_HEADS * KPAD).astype(BF16)

    k_cat, v_t, q_t = pl.pallas_call(
        _proj_kernel,
        grid=(b, n_s),
        in_specs=[tile, pl.BlockSpec((tm, 2 * ROPE), lambda i, j: (j, 0)),
                  _const_spec((1, d)), _const_spec((d, c_rank + 2 * ROPE)), _const_spec((1, c_rank)),
                  _const_spec((c_rank, N_HEADS * NOPE)), _const_spec((c_rank, N_HEADS * V_DIM)),
                  _const_spec((1, d)), _const_spec((d, q_rank)), _const_spec((1, q_rank)),
                  _const_spec((q_rank, N_HEADS * KPAD))],
        out_specs=[pl.BlockSpec((1, N_HEADS, tm, KPAD), lambda i, j: (i, 0, j, 0)),
                   pl.BlockSpec((1, N_HEADS, tm // TK, V_DIM, TK), lambda i, j: (i, 0, j, 0, 0)),
                   pl.BlockSpec((1, N_HEADS, tm // TQ, KPAD, TQ), lambda i, j: (i, 0, j, 0, 0))],
        out_shape=[jax.ShapeDtypeStruct((b, N_HEADS, s, KPAD), BF16),
                   jax.ShapeDtypeStruct((b, N_HEADS, s // TK, V_DIM, TK), BF16),
                   jax.ShapeDtypeStruct((b, N_HEADS, s // TQ, KPAD, TQ), BF16)],
        compiler_params=_params(2),
        name="proj",
    )(h2, cs, _row(kv_in_g), w_dkv_kr, _row(kv_norm_g), kv_w_uk.astype(BF16), kv_w_uv.astype(BF16),
      _row(mix_pre_g[1]), mla_w_dq[0].astype(BF16), _row(mla_q_norm_g[0]), w_uq)

    nq = s // TQ
    assert nq % ATTN_GROUP == 0 and (nq * (nq - 1) // 2) % ATTN_GROUP == 0
    n_slots = 2 * ATTN_GROUP
    items = [(qi, qi) for qi in range(nq)] + [(qi, j) for qi in range(nq) for j in range(qi)]
    qtab = jnp.asarray([qi for qi, _ in items], jnp.int32)
    jtab = jnp.asarray([j for _, j in items], jnp.int32)
    key_chunk = lax.broadcasted_iota(jnp.int32, (TK, TQ), 0) // CHUNK
    qry_chunk = lax.broadcasted_iota(jnp.int32, (TK, TQ), 1) // CHUNK
    bias = jnp.where(key_chunk <= qry_chunk, 0.0, -jnp.inf).astype(F32)
    o_t = pl.pallas_call(
        _attn_kernel,
        grid_spec=pltpu.PrefetchScalarGridSpec(
            num_scalar_prefetch=2,
            grid=(b, N_HEADS),
            in_specs=[pl.BlockSpec((1, 1, nq, KPAD, TQ), lambda i, h, *_: (i, h, 0, 0, 0)),
                      pl.BlockSpec((1, 1, s, KPAD), lambda i, h, *_: (i, h, 0, 0)),
                      pl.BlockSpec((1, 1, s // TK, V_DIM, TK), lambda i, h, *_: (i, h, 0, 0, 0)),
                      _const_spec((TK, TQ))],
            out_specs=pl.BlockSpec((1, 1, nq, V_DIM, TQ), lambda i, h, *_: (i, h, 0, 0, 0)),
            scratch_shapes=[pltpu.VMEM((n_slots, TK, TQ), F32), pltpu.VMEM((n_slots, 1, TQ), F32),
                            pltpu.VMEM((n_slots, TK, TQ), BF16), pltpu.VMEM((n_slots, 1, TQ), F32),
                            pltpu.VMEM((nq, 1, TQ), F32), pltpu.VMEM((nq, 1, TQ), F32),
                            pltpu.VMEM((nq, V_DIM, TQ), F32)]),
        out_shape=jax.ShapeDtypeStruct((b, N_HEADS, nq, V_DIM, TQ), BF16),
        compiler_params=_params(2),
        name="attention",
    )(qtab, jtab, q_t, k_cat, v_t, bias)

    return pl.pallas_call(
        _out_ffn_kernel,
        grid=(b, n_s),
        in_specs=[pl.BlockSpec((1, N_HEADS, tm // TQ, V_DIM, TQ), lambda i, j: (i, 0, j, 0, 0)), tile,
                  _const_spec((N_HEADS * V_DIM, d)), _const_spec((1, d)), _const_spec((1, d)),
                  _const_spec((d, d_ff)), _const_spec((d_ff, d)), _const_spec((1, d))],
        out_specs=tile,
        out_shape=jax.ShapeDtypeStruct((b, s, d), F32),
        scratch_shapes=[pltpu.VMEM((tm, N_HEADS * V_DIM), BF16)],
        compiler_params=_params(2),
        name="out_ffn",
    )(o_t, h2, mla_w_o[0].astype(BF16), _row(mix_post_g[1]), _row(ffn_pre_g[1]),
      w_ff1[1].astype(BF16), w_ff2[1].astype(BF16), _row(ffn_post_g[1]))
```

```python
import functools

import jax
import jax.numpy as jnp
from jax import lax
from jax.experimental import pallas as pl
from jax.experimental.pallas import tpu as pltpu

F32 = jnp.float32
BF16 = jnp.bfloat16

RMS_EPS = 1e-6
LN_EPS = 1e-5
ROPE_BASE = 10000.0
CHUNK = 64
CONV_WIDTH = 31
N_HEADS = 8
NOPE = 128
ROPE = 64
V_DIM = 128
QK_DIM = NOPE + ROPE
KPAD = 256
LOG2_E = 1.4426950408889634
Q_SCALE = QK_DIM ** -0.5 * LOG2_E

SUBLANES = 8
HALO = 32
CONV_ROWS = 64
CONV_LANES = 256
FFN_COLS = 1024
TQ = 256
TK = 256
ATTN_GROUP = 4
ATTN_UNROLL = 14
VMEM_LIMIT = 56 * 1024 * 1024


def _rms_scale(x):
    return x * lax.rsqrt(jnp.mean(x * x, axis=-1, keepdims=True) + RMS_EPS)


def _ffn(h, g_pre, w1_ref, w2_ref, g_post):
    hn = (_rms_scale(h) * g_pre).astype(BF16)
    d_ff = w1_ref.shape[1]
    acc = None
    for c in range(d_ff // FFN_COLS):
        cols = slice(c * FFN_COLS, (c + 1) * FFN_COLS)
        a = jnp.maximum(jnp.dot(hn, w1_ref[:, cols], preferred_element_type=F32), 0.0)
        t = jnp.dot((a * a).astype(BF16), w2_ref[cols, :], preferred_element_type=F32)
        acc = t if acc is None else acc + t
    return h + _rms_scale(acc) * g_post


def _mixer_ffn_kernel(xa_ref, xb_ref, gpre_ref, wpw1_ref, bpw1_ref, wdw_ref, bdw_ref, lng_ref, lnb_ref,
                      wpw2_ref, bpw2_ref, gmix_ref, gffn_ref, w1_ref, w2_ref, gpost_ref,
                      o_ref, buf_ref, cv_ref, *, tiles_per_seq):
    g = pl.program_id(0)
    tm, d = xa_ref.shape

    @pl.when(g == 0)
    def _no_previous_tile():
        cv_ref[...] = jnp.zeros(cv_ref.shape, F32)

    @pl.when(lax.rem(g, tiles_per_seq) == 0)
    def _zero_history():
        buf_ref[0:HALO, :] = jnp.zeros((HALO, d), F32)

    hn = (_rms_scale(xa_ref[...]) * gpre_ref[...]).astype(BF16)
    u = jnp.dot(hn, wpw1_ref[...], preferred_element_type=F32) + bpw1_ref[...]
    buf_ref[HALO:HALO + tm, :] = u[:, :d] * jax.nn.sigmoid(u[:, d:])

    cv = cv_ref[...]
    xc = cv - jnp.mean(cv, axis=-1, keepdims=True)
    y = xc * lax.rsqrt(jnp.mean(xc * xc, axis=-1, keepdims=True) + LN_EPS) * lng_ref[...] + lnb_ref[...]
    y = (y * jax.nn.sigmoid(y)).astype(BF16)
    m = jnp.dot(y, wpw2_ref[...], preferred_element_type=F32) + bpw2_ref[...]
    h1 = xb_ref[...] + _rms_scale(m) * gmix_ref[...]
    hn1 = (_rms_scale(h1) * gffn_ref[...]).astype(BF16)

    first_off = HALO - (CONV_WIDTH - 1)

    def zero_after(v):
        bits = pltpu.bitcast(v, jnp.int32)
        half = jnp.full(bits.shape, 16, jnp.int32)
        return lax.shift_right_logical(lax.shift_right_logical(bits, half), half).astype(F32)

    def conv_unit(r0, c, pace):
        lanes = slice(c * CONV_LANES, (c + 1) * CONV_LANES)
        out = bdw_ref[:, lanes] + zero_after(pace)
        out = jnp.broadcast_to(out[0:1], (CONV_ROWS, CONV_LANES))
        for r in range(SUBLANES):
            rows = CONV_ROWS + (SUBLANES if r else 0)
            group = None
            for off in range(r, HALO + 1, SUBLANES):
                if off < first_off:
                    continue
                k = off - first_off
                start = r0 + off - r
                term = wdw_ref[k:k + 1, lanes] * buf_ref[start:start + rows, lanes]
                group = term if group is None else group + term
            out = out + group[r:r + CONV_ROWS]
        cv_ref[r0:r0 + CONV_ROWS, lanes] = out

    n_slabs = w1_ref.shape[1] // FFN_COLS
    n_lane_chunks = d // CONV_LANES
    units = [(r0, c) for r0 in range(0, tm, CONV_ROWS) for c in range(n_lane_chunks)]

    per_slab = -(-len(units) // n_slabs)
    acc = None
    for s in range(n_slabs):
        cols = slice(s * FFN_COLS, (s + 1) * FFN_COLS)
        a = jnp.maximum(jnp.dot(hn1, w1_ref[:, cols], preferred_element_type=F32), 0.0)
        t = jnp.dot((a * a).astype(BF16), w2_ref[cols, :], preferred_element_type=F32)
        acc = t if acc is None else acc + t
        slab_units = units[s * per_slab:(s + 1) * per_slab]
        for i, (r0, c) in enumerate(slab_units):
            src = a if 2 * i < len(slab_units) else t
            row = ((2 * i) % len(slab_units)) * (tm // len(slab_units))
            conv_unit(r0, c, src[row:row + SUBLANES, 0:CONV_LANES])
    o_ref[...] = h1 + _rms_scale(acc) * gpost_ref[...]
    buf_ref[0:HALO, :] = buf_ref[tm:tm + HALO, :]


def _proj_kernel(h_ref, cs_ref, gkv_ref, wdkv_ref, gckv_ref, wuk_ref, wuv_ref,
                 gq_ref, wdq_ref, gcq_ref, wuq_ref, k_ref, vt_ref, qt_ref):
    tm = h_ref.shape[1]
    hs = _rms_scale(h_ref[0])
    cs = cs_ref[...]
    c_rank = gckv_ref.shape[1]
    lane = lax.broadcasted_iota(jnp.int32, (tm, 2 * ROPE), 1)

    ckv = jnp.dot((hs * gkv_ref[...]).astype(BF16), wdkv_ref[...], preferred_element_type=F32)
    c_kv = (_rms_scale(ckv[:, :c_rank]) * gckv_ref[...]).astype(BF16)
    kr = ckv[:, c_rank:] * cs
    kr = kr + pltpu.roll(kr, ROPE, 1)
    kr = jnp.where(lane < ROPE, kr, 0.0).astype(BF16)
    kn = jnp.dot(c_kv, wuk_ref[...], preferred_element_type=F32)
    v = jnp.dot(c_kv, wuv_ref[...], preferred_element_type=F32)
    for hd in range(N_HEADS):
        k_ref[0, hd, :, 0:NOPE] = kn[:, hd * NOPE:(hd + 1) * NOPE].astype(BF16)
        k_ref[0, hd, :, NOPE:KPAD] = kr
        for blk in range(tm // TK):
            vh = v[blk * TK:(blk + 1) * TK, hd * V_DIM:(hd + 1) * V_DIM]
            vt_ref[0, hd, blk] = vh.T.astype(BF16)

    cq = jnp.dot((hs * gq_ref[...]).astype(BF16), wdq_ref[...], preferred_element_type=F32)
    cq = (_rms_scale(cq) * gcq_ref[...]).astype(BF16)
    q = jnp.dot(cq, wuq_ref[...], preferred_element_type=F32) * Q_SCALE
    for hd in range(N_HEADS):
        qn = q[:, hd * KPAD:hd * KPAD + NOPE]
        qr = q[:, hd * KPAD + NOPE:(hd + 1) * KPAD] * cs
        qr = qr + pltpu.roll(qr, ROPE, 1)
        for blk in range(tm // TQ):
            rows = slice(blk * TQ, (blk + 1) * TQ)
            qt_ref[0, hd, blk, 0:NOPE, :] = qn[rows].T.astype(BF16)
            qt_ref[0, hd, blk, NOPE:KPAD, :] = qr[rows].T.astype(BF16)


def _attn_kernel(qtab_ref, jtab_ref, qt_ref, k_ref, vt_ref, bias_ref, ot_ref,
                 s_ref, mx_ref, p_ref, al_ref, m_ref, l_ref, acc_ref):
    nq = qt_ref.shape[2]
    n_items = qtab_ref.shape[0]
    n_groups = n_items // ATTN_GROUP
    n_diag_groups = nq // ATTN_GROUP

    def qk(t, slot, first):
        qi, j = qtab_ref[t], jtab_ref[t]
        k = k_ref[0, 0, pl.ds(pl.multiple_of(j * TK, TK), TK), :]
        s = jnp.dot(k, qt_ref[0, 0, qi], preferred_element_type=F32)
        if first:
            s = s + bias_ref[...]
        s_ref[slot] = s
        mx_ref[slot] = jnp.max(s, axis=0, keepdims=True)

    def softmax(t, slot, first):
        qi = qtab_ref[t]
        if first:
            m_new = mx_ref[slot]
            p = jnp.exp2(s_ref[slot] - m_new)
            l_ref[qi] = jnp.sum(p, axis=0, keepdims=True)
        else:
            m_old = m_ref[qi]
            m_new = jnp.maximum(m_old, mx_ref[slot])
            alpha = jnp.exp2(m_old - m_new)
            p = jnp.exp2(s_ref[slot] - m_new)
            l_ref[qi] = alpha * l_ref[qi] + jnp.sum(p, axis=0, keepdims=True)
            al_ref[slot] = alpha
        m_ref[qi] = m_new
        p_ref[slot] = p.astype(BF16)

    def pv(t, slot, first):
        qi, j = qtab_ref[t], jtab_ref[t]
        pv_t = jnp.dot(vt_ref[0, 0, j], p_ref[slot], preferred_element_type=F32)
        acc_ref[qi] = pv_t if first else al_ref[slot] * acc_ref[qi] + pv_t

    def body(g, parity, diag_qk, diag_pv, diag_sm):
        stages = ((qk, g, parity, diag_qk), (pv, g - 2, parity, diag_pv), (softmax, g - 1, 1 - parity, diag_sm))
        for stage, group, par, diag in stages:
            if diag is not None:
                for i in range(ATTN_GROUP):
                    stage(group * ATTN_GROUP + i, par * ATTN_GROUP + i, diag)

    def static_body(g):
        flag = lambda grp: (grp < n_diag_groups) if 0 <= grp < n_groups else None
        body(g, g % 2, flag(g), flag(g - 2), flag(g - 1))

    loop_lo = n_diag_groups + 2
    n_iters = max(n_groups - loop_lo, 0) // ATTN_UNROLL
    loop_hi = loop_lo + ATTN_UNROLL * n_iters
    for g in range(0, min(loop_lo, n_groups + 2)):
        static_body(g)

    def body_run(u, carry):
        for i in range(ATTN_UNROLL):
            body(loop_lo + ATTN_UNROLL * u + i, (loop_lo + i) % 2, False, False, False)
        return carry

    if n_iters:
        lax.fori_loop(0, n_iters, body_run, 0)
    for g in range(max(loop_hi, loop_lo), n_groups + 2):
        static_body(g)

    def finish(qi, carry):
        ot_ref[0, 0, qi] = (acc_ref[qi] * (1.0 / l_ref[qi])).astype(BF16)
        return carry

    lax.fori_loop(0, nq, finish, 0)


def _out_ffn_kernel(ot_ref, h_ref, wo_ref, gmix_ref, gpre_ref, w1_ref, w2_ref, gpost_ref, out_ref, o_scr):
    for hd in range(ot_ref.shape[1]):
        for blk in range(ot_ref.shape[2]):
            o_t = ot_ref[0, hd, blk].astype(F32)
            o_scr[blk * TQ:(blk + 1) * TQ, hd * V_DIM:(hd + 1) * V_DIM] = o_t.T.astype(BF16)
    m = jnp.dot(o_scr[...], wo_ref[...], preferred_element_type=F32)
    h = h_ref[0] + _rms_scale(m) * gmix_ref[...]
    out_ref[0] = _ffn(h, gpre_ref[...], w1_ref, w2_ref, gpost_ref[...])


def _const_spec(shape):
    zeros = (0,) * len(shape)
    return pl.BlockSpec(shape, lambda *_: zeros, pipeline_mode=pl.Buffered(1))


def _row(v):
    return v.reshape(1, -1).astype(F32)


def _rot_half_cols(w):
    half = w.shape[-1] // 2
    return jnp.concatenate([-w[..., half:], w[..., :half]], axis=-1)


def _params(n_grid_axes, flags=None):
    return pltpu.CompilerParams(dimension_semantics=("arbitrary",) * n_grid_axes,
                                vmem_limit_bytes=VMEM_LIMIT, flags=flags)


def kernel(x, mix_pre_g, mix_post_g, ffn_pre_g, ffn_post_g, w_ff1, w_ff2, conv_w_pw1, conv_b_pw1, conv_w_dw, conv_b_dw, conv_ln_g, conv_ln_b, conv_w_pw2, conv_b_pw2, mla_w_dq, mla_q_norm_g, mla_w_uq, mla_w_o, kv_in_g, kv_w_dkv, kv_norm_g, kv_w_kr, kv_w_uk, kv_w_uv):
    b, s, d = x.shape
    assert mix_pre_g.shape[0] == 2 and conv_w_pw1.shape[0] == 1 and mla_w_dq.shape[0] == 1
    assert conv_w_dw.shape[1] == CONV_WIDTH
    tm = min(512, s)
    assert s % tm == 0 and tm % TQ == 0 and TQ == TK and d % CONV_LANES == 0
    n_s = s // tm
    d_ff = w_ff1.shape[2]
    q_rank = mla_w_dq.shape[2]
    c_rank = kv_w_dkv.shape[1]

    tile = pl.BlockSpec((1, tm, d), lambda i, j: (i, j, 0))

    n_t = b * n_s
    x2 = x.reshape(n_t * tm, d)
    h2 = pl.pallas_call(
        functools.partial(_mixer_ffn_kernel, tiles_per_seq=n_s),
        grid=(n_t + 1,),
        in_specs=[pl.BlockSpec((tm, d), lambda g: (jnp.minimum(g, n_t - 1), 0)),
                  pl.BlockSpec((tm, d), lambda g: (jnp.maximum(g - 1, 0), 0)),
                  _const_spec((1, d)), _const_spec((d, 2 * d)), _const_spec((1, 2 * d)),
                  _const_spec((CONV_WIDTH, d)), _const_spec((1, d)), _const_spec((1, d)),
                  _const_spec((1, d)), _const_spec((d, d)), _const_spec((1, d)), _const_spec((1, d)),
                  _const_spec((1, d)), _const_spec((d, d_ff)), _const_spec((d_ff, d)), _const_spec((1, d))],
        out_specs=pl.BlockSpec((tm, d), lambda g: (jnp.maximum(g - 1, 0), 0)),
        out_shape=jax.ShapeDtypeStruct((n_t * tm, d), F32),
        scratch_shapes=[pltpu.VMEM((tm + HALO, d), F32), pltpu.VMEM((tm, d), F32)],
        compiler_params=_params(1),
        name="mixer_ffn",
    )(x2, x2, _row(mix_pre_g[0]), conv_w_pw1[0].astype(BF16), _row(conv_b_pw1[0]),
      conv_w_dw[0].astype(F32), _row(conv_b_dw[0]), _row(conv_ln_g[0]), _row(conv_ln_b[0]),
      conv_w_pw2[0].astype(BF16), _row(conv_b_pw2[0]), _row(mix_post_g[0]),
      _row(ffn_pre_g[0]), w_ff1[0].astype(BF16), w_ff2[0].astype(BF16), _row(ffn_post_g[0]))
    h2 = h2.reshape(b, s, d)

    inv = 1.0 / (ROPE_BASE ** (jnp.arange(0, ROPE, 2, dtype=F32) / ROPE))
    ang = jnp.arange(s, dtype=F32)[:, None] * inv[None, :]
    cs = jnp.concatenate([jnp.cos(ang), jnp.cos(ang), jnp.sin(ang), jnp.sin(ang)], axis=-1)

    w_dkv_kr = jnp.concatenate([kv_w_dkv, kv_w_kr, _rot_half_cols(kv_w_kr)], axis=-1).astype(BF16)
    w_uq = mla_w_uq[0].reshape(q_rank, N_HEADS, QK_DIM)
    w_uq = jnp.concatenate([w_uq, _rot_half_cols(w_uq[..., NOPE:])], axis=-1)
    w_uq = w_uq.reshape(q_rank, N_HEADS * KPAD).astype(BF16)

    k_cat, v_t, q_t = pl.pallas_call(
        _proj_kernel,
        grid=(b, n_s),
        in_specs=[tile, pl.BlockSpec((tm, 2 * ROPE), lambda i, j: (j, 0)),
                  _const_spec((1, d)), _const_spec((d, c_rank + 2 * ROPE)), _const_spec((1, c_rank)),
                  _const_spec((c_rank, N_HEADS * NOPE)), _const_spec((c_rank, N_HEADS * V_DIM)),
                  _const_spec((1, d)), _const_spec((d, q_rank)), _const_spec((1, q_rank)),
                  _const_spec((q_rank, N_HEADS * KPAD))],
        out_specs=[pl.BlockSpec((1, N_HEADS, tm, KPAD), lambda i, j: (i, 0, j, 0)),
                   pl.BlockSpec((1, N_HEADS, tm // TK, V_DIM, TK), lambda i, j: (i, 0, j, 0, 0)),
                   pl.BlockSpec((1, N_HEADS, tm // TQ, KPAD, TQ), lambda i, j: (i, 0, j, 0, 0))],
        out_shape=[jax.ShapeDtypeStruct((b, N_HEADS, s, KPAD), BF16),
                   jax.ShapeDtypeStruct((b, N_HEADS, s // TK, V_DIM, TK), BF16),
                   jax.ShapeDtypeStruct((b, N_HEADS, s // TQ, KPAD, TQ), BF16)],
        compiler_params=_params(2),
        name="proj",
    )(h2, cs, _row(kv_in_g), w_dkv_kr, _row(kv_norm_g), kv_w_uk.astype(BF16), kv_w_uv.astype(BF16),
      _row(mix_pre_g[1]), mla_w_dq[0].astype(BF16), _row(mla_q_norm_g[0]), w_uq)

    nq = s // TQ
    assert nq % ATTN_GROUP == 0 and (nq * (nq - 1) // 2) % ATTN_GROUP == 0
    n_slots = 2 * ATTN_GROUP
    items = [(qi, qi) for qi in range(nq)] + [(qi, j) for qi in range(nq) for j in range(qi)]
    qtab = jnp.asarray([qi for qi, _ in items], jnp.int32)
    jtab = jnp.asarray([j for _, j in items], jnp.int32)
    key_chunk = lax.broadcasted_iota(jnp.int32, (TK, TQ), 0) // CHUNK
    qry_chunk = lax.broadcasted_iota(jnp.int32, (TK, TQ), 1) // CHUNK
    bias = jnp.where(key_chunk <= qry_chunk, 0.0, -jnp.inf).astype(F32)
    o_t = pl.pallas_call(
        _attn_kernel,
        grid_spec=pltpu.PrefetchScalarGridSpec(
            num_scalar_prefetch=2,
            grid=(b, N_HEADS),
            in_specs=[pl.BlockSpec((1, 1, nq, KPAD, TQ), lambda i, h, *_: (i, h, 0, 0, 0)),
                      pl.BlockSpec((1, 1, s, KPAD), lambda i, h, *_: (i, h, 0, 0)),
                      pl.BlockSpec((1, 1, s // TK, V_DIM, TK), lambda i, h, *_: (i, h, 0, 0, 0)),
                      _const_spec((TK, TQ))],
            out_specs=pl.BlockSpec((1, 1, nq, V_DIM, TQ), lambda i, h, *_: (i, h, 0, 0, 0)),
            scratch_shapes=[pltpu.VMEM((n_slots, TK, TQ), F32), pltpu.VMEM((n_slots, 1, TQ), F32),
                            pltpu.VMEM((n_slots, TK, TQ), BF16), pltpu.VMEM((n_slots, 1, TQ), F32),
                            pltpu.VMEM((nq, 1, TQ), F32), pltpu.VMEM((nq, 1, TQ), F32),
                            pltpu.VMEM((nq, V_DIM, TQ), F32)]),
        out_shape=jax.ShapeDtypeStruct((b, N_HEADS, nq, V_DIM, TQ), BF16),
        compiler_params=_params(2),
        name="attention",
    )(qtab, jtab, q_t, k_cat, v_t, bias)

    return pl.pallas_call(
        _out_ffn_kernel,
        grid=(b, n_s),
        in_specs=[pl.BlockSpec((1, N_HEADS, tm // TQ, V_DIM, TQ), lambda i, j: (i, 0, j, 0, 0)), tile,
                  _const_spec((N_HEADS * V_DIM, d)), _const_spec((1, d)), _const_spec((1, d)),
                  _const_spec((d, d_ff)), _const_spec((d_ff, d)), _const_spec((1, d))],
        out_specs=tile,
        out_shape=jax.ShapeDtypeStruct((b, s, d), F32),
        scratch_shapes=[pltpu.VMEM((tm, N_HEADS * V_DIM), BF16)],
        compiler_params=_params(2),
        name="out_ffn",
    )(o_t, h2, mla_w_o[0].astype(BF16), _row(mix_post_g[1]), _row(ffn_pre_g[1]),
      w_ff1[1].astype(BF16), w_ff2[1].astype(BF16), _row(ffn_post_g[1]))
```

```python
import functools

import jax
import jax.numpy as jnp
from jax import lax
from jax.experimental import pallas as pl
from jax.experimental.pallas import tpu as pltpu

F32 = jnp.float32
BF16 = jnp.bfloat16

RMS_EPS = 1e-6
LN_EPS = 1e-5
ROPE_BASE = 10000.0
CHUNK = 64
CONV_WIDTH = 31
N_HEADS = 8
NOPE = 128
ROPE = 64
V_DIM = 128
QK_DIM = NOPE + ROPE
KPAD = 256
LOG2_E = 1.4426950408889634
Q_SCALE = QK_DIM ** -0.5 * LOG2_E

SUBLANES = 8
HALO = 32
CONV_ROWS = 64
CONV_LANES = 256
FFN_COLS = 1024
TQ = 256
TK = 256
ATTN_GROUP = 4
ATTN_UNROLL = 28
VMEM_LIMIT = 56 * 1024 * 1024


def _rms_scale(x):
    return x * lax.rsqrt(jnp.mean(x * x, axis=-1, keepdims=True) + RMS_EPS)


def _ffn(h, g_pre, w1_ref, w2_ref, g_post):
    hn = (_rms_scale(h) * g_pre).astype(BF16)
    d_ff = w1_ref.shape[1]
    acc = None
    for c in range(d_ff // FFN_COLS):
        cols = slice(c * FFN_COLS, (c + 1) * FFN_COLS)
        a = jnp.maximum(jnp.dot(hn, w1_ref[:, cols], preferred_element_type=F32), 0.0)
        t = jnp.dot((a * a).astype(BF16), w2_ref[cols, :], preferred_element_type=F32)
        acc = t if acc is None else acc + t
    return h + _rms_scale(acc) * g_post


def _mixer_ffn_kernel(xa_ref, xb_ref, gpre_ref, wpw1_ref, bpw1_ref, wdw_ref, bdw_ref, lng_ref, lnb_ref,
                      wpw2_ref, bpw2_ref, gmix_ref, gffn_ref, w1_ref, w2_ref, gpost_ref,
                      o_ref, buf_ref, cv_ref, *, tiles_per_seq):
    g = pl.program_id(0)
    tm, d = xa_ref.shape

    @pl.when(g == 0)
    def _no_previous_tile():
        cv_ref[...] = jnp.zeros(cv_ref.shape, F32)

    @pl.when(lax.rem(g, tiles_per_seq) == 0)
    def _zero_history():
        buf_ref[0:HALO, :] = jnp.zeros((HALO, d), F32)

    hn = (_rms_scale(xa_ref[...]) * gpre_ref[...]).astype(BF16)
    u = jnp.dot(hn, wpw1_ref[...], preferred_element_type=F32) + bpw1_ref[...]
    buf_ref[HALO:HALO + tm, :] = u[:, :d] * jax.nn.sigmoid(u[:, d:])

    cv = cv_ref[...]
    xc = cv - jnp.mean(cv, axis=-1, keepdims=True)
    y = xc * lax.rsqrt(jnp.mean(xc * xc, axis=-1, keepdims=True) + LN_EPS) * lng_ref[...] + lnb_ref[...]
    y = (y * jax.nn.sigmoid(y)).astype(BF16)
    m = jnp.dot(y, wpw2_ref[...], preferred_element_type=F32) + bpw2_ref[...]
    h1 = xb_ref[...] + _rms_scale(m) * gmix_ref[...]
    hn1 = (_rms_scale(h1) * gffn_ref[...]).astype(BF16)

    first_off = HALO - (CONV_WIDTH - 1)

    def zero_after(v):
        bits = pltpu.bitcast(v, jnp.int32)
        half = jnp.full(bits.shape, 16, jnp.int32)
        return lax.shift_right_logical(lax.shift_right_logical(bits, half), half).astype(F32)

    def conv_unit(r0, c, pace):
        lanes = slice(c * CONV_LANES, (c + 1) * CONV_LANES)
        out = bdw_ref[:, lanes] + zero_after(pace)
        out = jnp.broadcast_to(out[0:1], (CONV_ROWS, CONV_LANES))
        for r in range(SUBLANES):
            rows = CONV_ROWS + (SUBLANES if r else 0)
            group = None
            for off in range(r, HALO + 1, SUBLANES):
                if off < first_off:
                    continue
                k = off - first_off
                start = r0 + off - r
                term = wdw_ref[k:k + 1, lanes] * buf_ref[start:start + rows, lanes]
                group = term if group is None else group + term
            out = out + group[r:r + CONV_ROWS]
        cv_ref[r0:r0 + CONV_ROWS, lanes] = out

    n_slabs = w1_ref.shape[1] // FFN_COLS
    n_lane_chunks = d // CONV_LANES
    units = [(r0, c) for r0 in range(0, tm, CONV_ROWS) for c in range(n_lane_chunks)]

    per_slab = -(-len(units) // n_slabs)
    acc = None
    for s in range(n_slabs):
        cols = slice(s * FFN_COLS, (s + 1) * FFN_COLS)
        a = jnp.maximum(jnp.dot(hn1, w1_ref[:, cols], preferred_element_type=F32), 0.0)
        t = jnp.dot((a * a).astype(BF16), w2_ref[cols, :], preferred_element_type=F32)
        acc = t if acc is None else acc + t
        slab_units = units[s * per_slab:(s + 1) * per_slab]
        for i, (r0, c) in enumerate(slab_units):
            src = a if 2 * i < len(slab_units) else t
            row = ((2 * i) % len(slab_units)) * (tm // len(slab_units))
            conv_unit(r0, c, src[row:row + SUBLANES, 0:CONV_LANES])
    o_ref[...] = h1 + _rms_scale(acc) * gpost_ref[...]
    buf_ref[0:HALO, :] = buf_ref[tm:tm + HALO, :]


def _proj_kernel(h_ref, cs_ref, gkv_ref, wdkv_ref, gckv_ref, wuk_ref, wuv_ref,
                 gq_ref, wdq_ref, gcq_ref, wuq_ref, k_ref, vt_ref, qt_ref):
    tm = h_ref.shape[1]
    hs = _rms_scale(h_ref[0])
    cs = cs_ref[...]
    c_rank = gckv_ref.shape[1]
    lane = lax.broadcasted_iota(jnp.int32, (tm, 2 * ROPE), 1)

    ckv = jnp.dot((hs * gkv_ref[...]).astype(BF16), wdkv_ref[...], preferred_element_type=F32)
    c_kv = (_rms_scale(ckv[:, :c_rank]) * gckv_ref[...]).astype(BF16)
    kr = ckv[:, c_rank:] * cs
    kr = kr + pltpu.roll(kr, ROPE, 1)
    kr = jnp.where(lane < ROPE, kr, 0.0).astype(BF16)
    kn = jnp.dot(c_kv, wuk_ref[...], preferred_element_type=F32)
    v = jnp.dot(c_kv, wuv_ref[...], preferred_element_type=F32)
    for hd in range(N_HEADS):
        k_ref[0, hd, :, 0:NOPE] = kn[:, hd * NOPE:(hd + 1) * NOPE].astype(BF16)
        k_ref[0, hd, :, NOPE:KPAD] = kr
        for blk in range(tm // TK):
            vh = v[blk * TK:(blk + 1) * TK, hd * V_DIM:(hd + 1) * V_DIM]
            vt_ref[0, hd, blk] = vh.T.astype(BF16)

    cq = jnp.dot((hs * gq_ref[...]).astype(BF16), wdq_ref[...], preferred_element_type=F32)
    cq = (_rms_scale(cq) * gcq_ref[...]).astype(BF16)
    q = jnp.dot(cq, wuq_ref[...], preferred_element_type=F32) * Q_SCALE
    for hd in range(N_HEADS):
        qn = q[:, hd * KPAD:hd * KPAD + NOPE]
        qr = q[:, hd * KPAD + NOPE:(hd + 1) * KPAD] * cs
        qr = qr + pltpu.roll(qr, ROPE, 1)
        for blk in range(tm // TQ):
            rows = slice(blk * TQ, (blk + 1) * TQ)
            qt_ref[0, hd, blk, 0:NOPE, :] = qn[rows].T.astype(BF16)
            qt_ref[0, hd, blk, NOPE:KPAD, :] = qr[rows].T.astype(BF16)


def _attn_kernel(qtab_ref, jtab_ref, qt_ref, k_ref, vt_ref, bias_ref, ot_ref,
                 s_ref, mx_ref, p_ref, al_ref, m_ref, l_ref, acc_ref, *, items):
    nq = qt_ref.shape[2]
    n_items = qtab_ref.shape[0]
    n_groups = n_items // ATTN_GROUP
    n_diag_groups = nq // ATTN_GROUP

    def item(t):
        if isinstance(t, int):
            return items[t]
        return qtab_ref[t], jtab_ref[t]

    def qk(t, slot, first):
        qi, j = item(t)
        row0 = j * TK if isinstance(j, int) else pl.multiple_of(j * TK, TK)
        k = k_ref[0, 0, pl.ds(row0, TK), :]
        s = jnp.dot(k, qt_ref[0, 0, qi], preferred_element_type=F32)
        if first:
            s = s + bias_ref[...]
        s_ref[slot] = s
        mx_ref[slot] = jnp.max(s, axis=0, keepdims=True)

    def softmax(t, slot, first):
        qi, _ = item(t)
        if first:
            m_new = mx_ref[slot]
            p = jnp.exp2(s_ref[slot] - m_new)
            l_ref[qi] = jnp.sum(p, axis=0, keepdims=True)
        else:
            m_old = m_ref[qi]
            m_new = jnp.maximum(m_old, mx_ref[slot])
            alpha = jnp.exp2(m_old - m_new)
            p = jnp.exp2(s_ref[slot] - m_new)
            l_ref[qi] = alpha * l_ref[qi] + jnp.sum(p, axis=0, keepdims=True)
            al_ref[slot] = alpha
        m_ref[qi] = m_new
        p_ref[slot] = p.astype(BF16)

    def pv(t, slot, first):
        qi, j = item(t)
        pv_t =jnp.dot(vt_ref[0, 0, j], p_ref[slot], preferred_element_type=F32)
        acc_ref[qi] = pv_t if first else al_ref[slot] * acc_ref[qi] + pv_t

    def body(g, parity, diag_qk, diag_pv, diag_sm):
        stages = ((qk, g, parity, diag_qk), (pv, g - 2, parity, diag_pv), (softmax, g - 1, 1 - parity, diag_sm))
        for stage, group, par, diag in stages:
            if diag is not None:
                for i in range(ATTN_GROUP):
                    stage(group * ATTN_GROUP + i, par * ATTN_GROUP + i, diag)

    def static_body(g):
        flag = lambda grp: (grp < n_diag_groups) if 0 <= grp < n_groups else None
        body(g, g % 2, flag(g), flag(g - 2), flag(g - 1))

    loop_lo = n_diag_groups + 2
    n_iters = max(n_groups - loop_lo, 0) // ATTN_UNROLL
    loop_hi = loop_lo + ATTN_UNROLL * n_iters
    for g in range(0, min(loop_lo, n_groups + 2)):
        static_body(g)

    def body_run(u, carry):
        for i in range(ATTN_UNROLL):
            body(loop_lo + ATTN_UNROLL * u + i, (loop_lo + i) % 2, False, False, False)
        return carry

    if n_iters == 1:
        for g in range(loop_lo, loop_hi):
            static_body(g)
    elif n_iters:
        lax.fori_loop(0, n_iters, body_run, 0)
    for g in range(max(loop_hi, loop_lo), n_groups + 2):
        static_body(g)

    def finish(qi, carry):
        ot_ref[0, 0, qi] = (acc_ref[qi] * (1.0 / l_ref[qi])).astype(BF16)
        return carry

    lax.fori_loop(0, nq, finish, 0)


def _out_ffn_kernel(ot_ref, h_ref, wo_ref, gmix_ref, gpre_ref, w1_ref, w2_ref, gpost_ref, out_ref, o_scr):
    for hd in range(ot_ref.shape[1]):
        for blk in range(ot_ref.shape[2]):
            o_t = ot_ref[0, hd, blk].astype(F32)
            o_scr[blk * TQ:(blk + 1) * TQ, hd * V_DIM:(hd + 1) * V_DIM] = o_t.T.astype(BF16)
    m = jnp.dot(o_scr[...], wo_ref[...], preferred_element_type=F32)
    h = h_ref[0] + _rms_scale(m) * gmix_ref[...]
    out_ref[0] = _ffn(h, gpre_ref[...], w1_ref, w2_ref, gpost_ref[...])


def _const_spec(shape):
    zeros = (0,) * len(shape)
    return pl.BlockSpec(shape, lambda *_: zeros, pipeline_mode=pl.Buffered(1))


def _row(v):
    return v.reshape(1, -1).astype(F32)


def _rot_half_cols(w):
    half = w.shape[-1] // 2
    return jnp.concatenate([-w[..., half:], w[..., :half]], axis=-1)


def _params(n_grid_axes, flags=None):
    return pltpu.CompilerParams(dimension_semantics=("arbitrary",) * n_grid_axes,
                                vmem_limit_bytes=VMEM_LIMIT, flags=flags)


def kernel(x, mix_pre_g, mix_post_g, ffn_pre_g, ffn_post_g, w_ff1, w_ff2, conv_w_pw1, conv_b_pw1, conv_w_dw, conv_b_dw, conv_ln_g, conv_ln_b, conv_w_pw2, conv_b_pw2, mla_w_dq, mla_q_norm_g, mla_w_uq, mla_w_o, kv_in_g, kv_w_dkv, kv_norm_g, kv_w_kr, kv_w_uk, kv_w_uv):
    b, s, d = x.shape
    assert mix_pre_g.shape[0] == 2 and conv_w_pw1.shape[0] == 1 and mla_w_dq.shape[0] == 1
    assert conv_w_dw.shape[1] == CONV_WIDTH
    tm = min(512, s)
    assert s % tm == 0 and tm % TQ == 0 and TQ == TK and d % CONV_LANES == 0
    n_s = s // tm
    d_ff = w_ff1.shape[2]
    q_rank = mla_w_dq.shape[2]
    c_rank = kv_w_dkv.shape[1]

    tile = pl.BlockSpec((1, tm, d), lambda i, j: (i, j, 0))

    n_t = b * n_s
    x2 = x.reshape(n_t * tm, d)
    h2 = pl.pallas_call(
        functools.partial(_mixer_ffn_kernel, tiles_per_seq=n_s),
        grid=(n_t + 1,),
        in_specs=[pl.BlockSpec((tm, d), lambda g: (jnp.minimum(g, n_t - 1), 0)),
                  pl.BlockSpec((tm, d), lambda g: (jnp.maximum(g - 1, 0), 0)),
                  _const_spec((1, d)), _const_spec((d, 2 * d)), _const_spec((1, 2 * d)),
                  _const_spec((CONV_WIDTH, d)), _const_spec((1, d)), _const_spec((1, d)),
                  _const_spec((1, d)), _const_spec((d, d)), _const_spec((1, d)), _const_spec((1, d)),
                  _const_spec((1, d)), _const_spec((d, d_ff)), _const_spec((d_ff, d)), _const_spec((1, d))],
        out_specs=pl.BlockSpec((tm, d), lambda g: (jnp.maximum(g - 1, 0), 0)),
        out_shape=jax.ShapeDtypeStruct((n_t * tm, d), F32),
        scratch_shapes=[pltpu.VMEM((tm + HALO, d), F32), pltpu.VMEM((tm, d), F32)],
        compiler_params=_params(1),
        name="mixer_ffn",
    )(x2, x2, _row(mix_pre_g[0]), conv_w_pw1[0].astype(BF16), _row(conv_b_pw1[0]),
      conv_w_dw[0].astype(F32), _row(conv_b_dw[0]), _row(conv_ln_g[0]), _row(conv_ln_b[0]),
      conv_w_pw2[0].astype(BF16), _row(conv_b_pw2[0]), _row(mix_post_g[0]),
      _row(ffn_pre_g[0]), w_ff1[0].astype(BF16), w_ff2[0].astype(BF16), _row(ffn_post_g[0]))
    h2 = h2.reshape(b, s, d)

    inv = 1.0 / (ROPE_BASE ** (jnp.arange(0, ROPE, 2, dtype=F32) / ROPE))
    ang = jnp.arange(s, dtype=F32)[:, None] * inv[None, :]
    cs = jnp.concatenate([jnp.cos(ang), jnp.cos(ang), jnp.sin(ang), jnp.sin(ang)], axis=-1)

    w_dkv_kr = jnp.concatenate([kv_w_dkv, kv_w_kr, _rot_half_cols(kv_w_kr)], axis=-1).astype(BF16)
    w_uq = mla_w_uq[0].reshape(q_rank, N_HEADS, QK_DIM)
    w_uq = jnp.concatenate([w_uq, _rot_half_cols(w_uq[..., NOPE:])], axis=-1)
    w_uq = w_uq.reshape(q_rank, N_HEADS * KPAD).astype(BF16)

    k_cat, v_t, q_t = pl.pallas_call(
        _proj_kernel,
        grid=(b, n_s),
        in_specs=[tile, pl.BlockSpec((tm, 2 * ROPE), lambda i, j: (j, 0)),
                  _const_spec((1, d)), _const_spec((d, c_rank + 2 * ROPE)), _const_spec((1, c_rank)),
                  _const_spec((c_rank, N_HEADS * NOPE)), _const_spec((c_rank, N_HEADS * V_DIM)),
                  _const_spec((1, d)), _const_spec((d, q_rank)), _const_spec((1, q_rank)),
                  _const_spec((q_rank, N_HEADS * KPAD))],
        out_specs=[pl.BlockSpec((1, N_HEADS, tm, KPAD), lambda i, j: (i, 0, j, 0)),
                   pl.BlockSpec((1, N_HEADS, tm // TK, V_DIM, TK), lambda i, j: (i, 0, j, 0, 0)),
                   pl.BlockSpec((1, N_HEADS, tm // TQ, KPAD, TQ), lambda i, j: (i, 0, j, 0, 0))],
        out_shape=[jax.ShapeDtypeStruct((b, N_HEADS, s, KPAD), BF16),
                   jax.ShapeDtypeStruct((b, N_HEADS, s // TK, V_DIM, TK), BF16),
                   jax.ShapeDtypeStruct((b, N_HEADS, s // TQ, KPAD, TQ), BF16)],
        compiler_params=_params(2),
        name="proj",
    )(h2, cs, _row(kv_in_g), w_dkv_kr, _row(kv_norm_g), kv_w_uk.astype(BF16), kv_w_uv.astype(BF16),
      _row(mix_pre_g[1]), mla_w_dq[0].astype(BF16), _row(mla_q_norm_g[0]), w_uq)

    nq = s // TQ
    assert nq % ATTN_GROUP == 0 and (nq * (nq - 1) // 2) % ATTN_GROUP == 0
    n_slots = 2 * ATTN_GROUP
    items = [(qi, qi) for qi in range(nq)] + [(qi, j) for qi in range(nq) for j in range(qi)]
    qtab = jnp.asarray([qi for qi, _ in items], jnp.int32)
    jtab = jnp.asarray([j for _, j in items], jnp.int32)
    key_chunk = lax.broadcasted_iota(jnp.int32, (TK, TQ), 0) // CHUNK
    qry_chunk = lax.broadcasted_iota(jnp.int32, (TK, TQ), 1) // CHUNK
    bias = jnp.where(key_chunk <= qry_chunk, 0.0, -jnp.inf).astype(F32)
    o_t = pl.pallas_call(
        functools.partial(_attn_kernel, items=tuple(items)),
        grid_spec=pltpu.PrefetchScalarGridSpec(
            num_scalar_prefetch=2,
            grid=(b, N_HEADS),
            in_specs=[pl.BlockSpec((1, 1, nq, KPAD, TQ), lambda i, h, *_: (i, h, 0, 0, 0)),
                      pl.BlockSpec((1, 1, s, KPAD), lambda i, h, *_: (i, h, 0, 0)),
                      pl.BlockSpec((1, 1, s // TK, V_DIM, TK), lambda i, h, *_: (i, h, 0, 0, 0)),
                      _const_spec((TK, TQ))],
            out_specs=pl.BlockSpec((1, 1, nq, V_DIM, TQ), lambda i, h, *_: (i, h, 0, 0, 0)),
            scratch_shapes=[pltpu.VMEM((n_slots, TK, TQ), F32), pltpu.VMEM((n_slots, 1, TQ), F32),
                            pltpu.VMEM((n_slots, TK, TQ), BF16), pltpu.VMEM((n_slots, 1, TQ), F32),
                            pltpu.VMEM((nq, 1, TQ), F32), pltpu.VMEM((nq, 1, TQ), F32),
                            pltpu.VMEM((nq, V_DIM, TQ), F32)]),
        out_shape=jax.ShapeDtypeStruct((b, N_HEADS, nq, V_DIM, TQ), BF16),
        compiler_params=_params(2),
        name="attention",
    )(qtab, jtab, q_t, k_cat, v_t, bias)

    return pl.pallas_call(
        _out_ffn_kernel,
        grid=(b, n_s),
        in_specs=[pl.BlockSpec((1, N_HEADS, tm // TQ, V_DIM, TQ), lambda i, j: (i, 0, j, 0, 0)), tile,
                  _const_spec((N_HEADS * V_DIM, d)), _const_spec((1, d)), _const_spec((1, d)),
                  _const_spec((d, d_ff)), _const_spec((d_ff, d)), _const_spec((1, d))],
        out_specs=tile,
        out_shape=jax.ShapeDtypeStruct((b, s, d), F32),
        scratch_shapes=[pltpu.VMEM((tm, N_HEADS * V_DIM), BF16)],
        compiler_params=_params(2),
        name="out_ffn",
    )(o_t, h2, mla_w_o[0].astype(BF16), _row(mix_post_g[1]), _row(ffn_pre_g[1]),
      w_ff1[1].astype(BF16), w_ff2[1].astype(BF16), _row(ffn_post_g[1]))
```

```python
import functools

import jax
import jax.numpy as jnp
from jax import lax
from jax.experimental import pallas as pl
from jax.experimental.pallas import tpu as pltpu

F32 = jnp.float32
BF16 = jnp.bfloat16

RMS_EPS = 1e-6
LN_EPS = 1e-5
ROPE_BASE = 10000.0
CHUNK = 64
CONV_WIDTH = 31
N_HEADS = 8
NOPE = 128
ROPE = 64
V_DIM = 128
V_AUG = V_DIM + 16
QK_DIM = NOPE + ROPE
KPAD = 256
LOG2_E = 1.4426950408889634
Q_SCALE = QK_DIM ** -0.5 * LOG2_E

SUBLANES = 8
HALO = 32
CONV_ROWS = 64
CONV_LANES = 256
FFN_COLS = 1024
TQ = 256
TK = 256
ATTN_GROUP = 2
ATTN_UNROLL = 58
VMEM_LIMIT = 56 * 1024 * 1024


def _rms_scale(x):
    return x * lax.rsqrt(jnp.mean(x * x, axis=-1, keepdims=True) + RMS_EPS)


def _ffn(h, g_pre, w1_ref, w2_ref, g_post):
    hn = (_rms_scale(h) * g_pre).astype(BF16)
    d_ff = w1_ref.shape[1]
    acc = None
    for c in range(d_ff // FFN_COLS):
        cols = slice(c * FFN_COLS, (c + 1) * FFN_COLS)
        a = jnp.maximum(jnp.dot(hn, w1_ref[:, cols], preferred_element_type=F32), 0.0)
        t = jnp.dot((a * a).astype(BF16), w2_ref[cols, :], preferred_element_type=F32)
        acc = t if acc is None else acc + t
    return h + _rms_scale(acc) * g_post


def _mixer_ffn_kernel(xa_ref, xb_ref, gpre_ref, wpw1_ref, bpw1_ref, wdw_ref, bdw_ref, lng_ref, lnb_ref,
                      wpw2_ref, bpw2_ref, gmix_ref, gffn_ref, w1_ref, w2_ref, gpost_ref,
                      o_ref, buf_ref, cv_ref, *, tiles_per_seq):
    g = pl.program_id(0)
    tm, d = xa_ref.shape

    @pl.when(g == 0)
    def _no_previous_tile():
        cv_ref[...] = jnp.zeros(cv_ref.shape, F32)

    @pl.when(lax.rem(g, tiles_per_seq) == 0)
    def _zero_history():
        buf_ref[0:HALO, :] = jnp.zeros((HALO, d), F32)

    hn = (_rms_scale(xa_ref[...]) * gpre_ref[...]).astype(BF16)
    u = jnp.dot(hn, wpw1_ref[...], preferred_element_type=F32) + bpw1_ref[...]
    buf_ref[HALO:HALO + tm, :] = u[:, :d] * jax.nn.sigmoid(u[:, d:])

    cv = cv_ref[...]
    xc = cv - jnp.mean(cv, axis=-1, keepdims=True)
    y = xc * lax.rsqrt(jnp.mean(xc * xc, axis=-1, keepdims=True) + LN_EPS) * lng_ref[...] + lnb_ref[...]
    y = (y * jax.nn.sigmoid(y)).astype(BF16)
    m = jnp.dot(y, wpw2_ref[...], preferred_element_type=F32) + bpw2_ref[...]
    h1 = xb_ref[...] + _rms_scale(m) * gmix_ref[...]
    hn1 = (_rms_scale(h1) * gffn_ref[...]).astype(BF16)

    first_off = HALO - (CONV_WIDTH - 1)

    def zero_after(v):
        bits = pltpu.bitcast(v, jnp.int32)
        half = jnp.full(bits.shape, 16, jnp.int32)
        return lax.shift_right_logical(lax.shift_right_logical(bits, half), half).astype(F32)

    def conv_unit(r0, c, pace):
        lanes = slice(c * CONV_LANES, (c + 1) * CONV_LANES)
        out = bdw_ref[:, lanes] + zero_after(pace)
        out = jnp.broadcast_to(out[0:1], (CONV_ROWS, CONV_LANES))
        for r in range(SUBLANES):
            rows = CONV_ROWS + (SUBLANES if r else 0)
            group = None
            for off in range(r, HALO + 1, SUBLANES):
                if off < first_off:
                    continue
                k = off - first_off
                start = r0 + off - r
                term = wdw_ref[k:k + 1, lanes] * buf_ref[start:start + rows, lanes]
                group = term if group is None else group + term
            out = out + group[r:r + CONV_ROWS]
        cv_ref[r0:r0 + CONV_ROWS, lanes] = out

    n_slabs = w1_ref.shape[1] // FFN_COLS
    n_lane_chunks = d // CONV_LANES
    units = [(r0, c) for r0 in range(0, tm, CONV_ROWS) for c in range(n_lane_chunks)]

    per_slab = -(-len(units) // n_slabs)
    acc = None
    for s in range(n_slabs):
        cols = slice(s * FFN_COLS, (s + 1) * FFN_COLS)
        a = jnp.maximum(jnp.dot(hn1, w1_ref[:, cols], preferred_element_type=F32), 0.0)
        t = jnp.dot((a * a).astype(BF16), w2_ref[cols, :], preferred_element_type=F32)
        acc = t if acc is None else acc + t
        slab_units = units[s * per_slab:(s + 1) * per_slab]
        for i, (r0, c) in enumerate(slab_units):
            src = a if 2 * i < len(slab_units) else t
            row = ((2 * i) % len(slab_units)) * (tm // len(slab_units))
            conv_unit(r0, c, src[row:row + SUBLANES, 0:CONV_LANES])
    o_ref[...] = h1 + _rms_scale(acc) * gpost_ref[...]
    buf_ref[0:HALO, :] = buf_ref[tm:tm + HALO, :]


def _proj_kernel(h_ref, cs_ref, gkv_ref, wdkv_ref, gckv_ref, wuk_ref, wuv_ref,
                 gq_ref, wdq_ref, gcq_ref, wuq_ref, k_ref, vt_ref, qt_ref):
    tm = h_ref.shape[1]
    hs = _rms_scale(h_ref[0])
    cs = cs_ref[...]
    c_rank = gckv_ref.shape[1]
    lane = lax.broadcasted_iota(jnp.int32, (tm, 2 * ROPE), 1)

    ckv = jnp.dot((hs * gkv_ref[...]).astype(BF16), wdkv_ref[...], preferred_element_type=F32)
    c_kv = (_rms_scale(ckv[:, :c_rank]) * gckv_ref[...]).astype(BF16)
    kr = ckv[:, c_rank:] * cs
    kr = kr + pltpu.roll(kr, ROPE, 1)
    kr = jnp.where(lane < ROPE, kr, 0.0).astype(BF16)
    kn = jnp.dot(c_kv, wuk_ref[...], preferred_element_type=F32)
    v = jnp.dot(c_kv, wuv_ref[...], preferred_element_type=F32)
    ones_row = (lax.broadcasted_iota(jnp.int32, (V_AUG - V_DIM, TK), 0) == 0).astype(BF16)
    for hd in range(N_HEADS):
        k_ref[0, hd, :, 0:NOPE] = kn[:, hd * NOPE:(hd + 1) * NOPE].astype(BF16)
        k_ref[0, hd, :, NOPE:KPAD] = kr
        for blk in range(tm // TK):
            vh = v[blk * TK:(blk + 1) * TK, hd * V_DIM:(hd + 1) * V_DIM]
            vt_ref[0, hd, blk, 0:V_DIM] = vh.T.astype(BF16)
            vt_ref[0, hd, blk, V_DIM:V_AUG] = ones_row

    cq = jnp.dot((hs * gq_ref[...]).astype(BF16), wdq_ref[...], preferred_element_type=F32)
    cq = (_rms_scale(cq) * gcq_ref[...]).astype(BF16)
    q = jnp.dot(cq, wuq_ref[...], preferred_element_type=F32) * Q_SCALE
    for hd in range(N_HEADS):
        qn = q[:, hd * KPAD:hd * KPAD + NOPE]
        qr = q[:, hd * KPAD + NOPE:(hd + 1) * KPAD] * cs
        qr = qr + pltpu.roll(qr, ROPE, 1)
        for blk in range(tm // TQ):
            rows = slice(blk * TQ, (blk + 1) * TQ)
            qt_ref[0, hd, blk, 0:NOPE, :] = qn[rows].T.astype(BF16)
            qt_ref[0, hd, blk, NOPE:KPAD, :] = qr[rows].T.astype(BF16)


def _attn_kernel(qtab_ref, jtab_ref, qt_ref, k_ref, vt_ref, bias_ref, ot_ref,
                 s_ref, mx_ref, p_ref, al_ref, m_ref, acc_ref, *, items):
    nq = qt_ref.shape[2]
    n_items = qtab_ref.shape[0]
    n_groups = n_items // ATTN_GROUP
    n_diag_groups = nq // ATTN_GROUP

    def item(t):
        if isinstance(t, int):
            return items[t]
        return qtab_ref[t], jtab_ref[t]

    def qk(t, slot, first):
        qi, j = item(t)
        row0 = j * TK if isinstance(j, int) else pl.multiple_of(j * TK, TK)
        k = k_ref[0, 0, pl.ds(row0, TK), :]
        s = jnp.dot(k, qt_ref[0, 0, qi], preferred_element_type=F32)
        if first:
            s = s + bias_ref[...]
        s_ref[slot] = s
        mx_ref[slot] = jnp.max(s, axis=0, keepdims=True)

    def softmax(t, slot, first):
        qi, _ = item(t)
        if first:
            m_new = mx_ref[slot]
            p = jnp.exp2(s_ref[slot] - m_new)
        else:
            m_old = m_ref[qi]
            m_new = jnp.maximum(m_old, mx_ref[slot])
            alpha = jnp.exp2(m_old - m_new)
            p = jnp.exp2(s_ref[slot] - m_new)
            al_ref[slot] = alpha
        m_ref[qi] = m_new
        p_ref[slot] = p.astype(BF16)

    def pv(t, slot, first):
        qi, j = item(t)
        pv_t = jnp.dot(vt_ref[0, 0, j], p_ref[slot], preferred_element_type=F32)
        acc_ref[qi] = pv_t if first else al_ref[slot] * acc_ref[qi] + pv_t

    def body(g, parity, diag_qk, diag_pv, diag_sm):
        stages = ((qk, g, parity, diag_qk), (pv, g - 2, parity, diag_pv), (softmax, g - 1, 1 - parity, diag_sm))
        for stage, group, par, diag in stages:
            if diag is not None:
                for i in range(ATTN_GROUP):
                    stage(group * ATTN_GROUP + i, par * ATTN_GROUP + i, diag)

    def static_body(g):
        flag = lambda grp: (grp < n_diag_groups) if 0 <= grp < n_groups else None
        body(g, g % 2, flag(g), flag(g - 2), flag(g - 1))

    loop_lo = n_diag_groups + 2
    n_iters = max(n_groups - loop_lo, 0) // ATTN_UNROLL
    loop_hi = loop_lo + ATTN_UNROLL * n_iters
    for g in range(0, min(loop_lo, n_groups + 2)):
        static_body(g)

    def body_run(u, carry):
        for i in range(ATTN_UNROLL):
            body(loop_lo + ATTN_UNROLL * u + i, (loop_lo + i) % 2, False, False, False)
        return carry

    if n_iters == 1:
        for g in range(loop_lo, loop_hi):
            static_body(g)
    elif n_iters:
        lax.fori_loop(0, n_iters, body_run, 0)
    for g in range(max(loop_hi, loop_lo), n_groups + 2):
        static_body(g)

    def finish(qi, carry):
        acc = acc_ref[qi]
        ot_ref[0, 0, qi] = (acc[0:V_DIM] * (1.0 / acc[V_DIM:V_DIM + 1])).astype(BF16)
        return carry

    lax.fori_loop(0, nq, finish, 0)


def _out_ffn_kernel(ot_ref, h_ref, wo_ref, gmix_ref, gpre_ref, w1_ref, w2_ref, gpost_ref, out_ref, o_scr):
    for hd in range(ot_ref.shape[1]):
        for blk in range(ot_ref.shape[2]):
            o_t = ot_ref[0, hd, blk].astype(F32)
            o_scr[blk * TQ:(blk + 1) * TQ, hd * V_DIM:(hd + 1) * V_DIM] = o_t.T.astype(BF16)
    m = jnp.dot(o_scr[...], wo_ref[...], preferred_element_type=F32)
    h = h_ref[0] + _rms_scale(m) * gmix_ref[...]
    out_ref[0] = _ffn(h, gpre_ref[...], w1_ref, w2_ref, gpost_ref[...])


def _const_spec(shape):
    zeros = (0,) * len(shape)
    return pl.BlockSpec(shape, lambda *_: zeros, pipeline_mode=pl.Buffered(1))


def _row(v):
    return v.reshape(1, -1).astype(F32)


def _rot_half_cols(w):
    half = w.shape[-1] // 2
    return jnp.concatenate([-w[..., half:], w[..., :half]], axis=-1)


def _params(n_grid_axes, flags=None):
    return pltpu.CompilerParams(dimension_semantics=("arbitrary",) * n_grid_axes,
                                vmem_limit_bytes=VMEM_LIMIT, flags=flags)


def kernel(x, mix_pre_g, mix_post_g, ffn_pre_g, ffn_post_g, w_ff1, w_ff2, conv_w_pw1, conv_b_pw1, conv_w_dw, conv_b_dw, conv_ln_g, conv_ln_b, conv_w_pw2, conv_b_pw2, mla_w_dq, mla_q_norm_g, mla_w_uq, mla_w_o, kv_in_g, kv_w_dkv, kv_norm_g, kv_w_kr, kv_w_uk, kv_w_uv):
    b, s, d = x.shape
    assert mix_pre_g.shape[0] == 2 and conv_w_pw1.shape[0] == 1 and mla_w_dq.shape[0] == 1
    assert conv_w_dw.shape[1] == CONV_WIDTH
    tm = min(512, s)
    assert s % tm == 0 and tm % TQ == 0 and TQ == TK and d % CONV_LANES == 0
    n_s = s // tm
    d_ff = w_ff1.shape[2]
    q_rank = mla_w_dq.shape[2]
    c_rank = kv_w_dkv.shape[1]

    tile = pl.BlockSpec((1, tm, d), lambda i, j: (i, j, 0))

    n_t = b * n_s
    x2 = x.reshape(n_t * tm, d)
    h2 = pl.pallas_call(
        functools.partial(_mixer_ffn_kernel, tiles_per_seq=n_s),
        grid=(n_t + 1,),
        in_specs=[pl.BlockSpec((tm, d), lambda g: (jnp.minimum(g, n_t - 1), 0)),
                  pl.BlockSpec((tm, d), lambda g: (jnp.maximum(g - 1, 0), 0)),
                  _const_spec((1, d)), _const_spec((d, 2 * d)), _const_spec((1, 2 * d)),
                  _const_spec((CONV_WIDTH, d)), _const_spec((1, d)), _const_spec((1, d)),
                  _const_spec((1, d)), _const_spec((d, d)), _const_spec((1, d)), _const_spec((1, d)),
                  _const_spec((1, d)), _const_spec((d, d_ff)), _const_spec((d_ff, d)), _const_spec((1, d))],
        out_specs=pl.BlockSpec((tm, d), lambda g: (jnp.maximum(g - 1, 0), 0)),
        out_shape=jax.ShapeDtypeStruct((n_t * tm, d), F32),
        scratch_shapes=[pltpu.VMEM((tm + HALO, d), F32), pltpu.VMEM((tm, d), F32)],
        compiler_params=_params(1),
        name="mixer_ffn",
    )(x2, x2, _row(mix_pre_g[0]), conv_w_pw1[0].astype(BF16), _row(conv_b_pw1[0]),
      conv_w_dw[0].astype(F32), _row(conv_b_dw[0]), _row(conv_ln_g[0]), _row(conv_ln_b[0]),
      conv_w_pw2[0].astype(BF16), _row(conv_b_pw2[0]), _row(mix_post_g[0]),
      _row(ffn_pre_g[0]), w_ff1[0].astype(BF16), w_ff2[0].astype(BF16), _row(ffn_post_g[0]))
    h2 = h2.reshape(b, s, d)

    inv = 1.0 / (ROPE_BASE ** (jnp.arange(0, ROPE, 2, dtype=F32) / ROPE))
    ang = jnp.arange(s, dtype=F32)[:, None] * inv[None, :]
    cs = jnp.concatenate([jnp.cos(ang), jnp.cos(ang), jnp.sin(ang), jnp.sin(ang)], axis=-1)

    w_dkv_kr = jnp.concatenate([kv_w_dkv, kv_w_kr, _rot_half_cols(kv_w_kr)], axis=-1).astype(BF16)
    w_uq = mla_w_uq[0].reshape(q_rank, N_HEADS, QK_DIM)
    w_uq = jnp.concatenate([w_uq, _rot_half_cols(w_uq[..., NOPE:])], axis=-1)
    w_uq = w_uq.reshape(q_rank, N_HEADS * KPAD).astype(BF16)

    k_cat, v_t, q_t = pl.pallas_call(
        _proj_kernel,
        grid=(b, n_s),
        in_specs=[tile, pl.BlockSpec((tm, 2 * ROPE), lambda i, j: (j, 0)),
                  _const_spec((1, d)), _const_spec((d, c_rank + 2 * ROPE)), _const_spec((1, c_rank)),
                  _const_spec((c_rank, N_HEADS * NOPE)), _const_spec((c_rank, N_HEADS * V_DIM)),
                  _const_spec((1, d)), _const_spec((d, q_rank)), _const_spec((1, q_rank)),
                  _const_spec((q_rank, N_HEADS * KPAD))],
        out_specs=[pl.BlockSpec((1, N_HEADS, tm, KPAD), lambda i, j: (i, 0, j, 0)),
                   pl.BlockSpec((1, N_HEADS, tm // TK, V_AUG, TK), lambda i, j: (i, 0, j, 0, 0)),
                   pl.BlockSpec((1, N_HEADS, tm // TQ, KPAD, TQ), lambda i, j: (i, 0, j, 0, 0))],
        out_shape=[jax.ShapeDtypeStruct((b, N_HEADS, s, KPAD), BF16),
                   jax.ShapeDtypeStruct((b, N_HEADS, s // TK, V_AUG, TK), BF16),
                   jax.ShapeDtypeStruct((b, N_HEADS, s // TQ, KPAD, TQ), BF16)],
        compiler_params=_params(2),
        name="proj",
    )(h2, cs, _row(kv_in_g), w_dkv_kr, _row(kv_norm_g), kv_w_uk.astype(BF16), kv_w_uv.astype(BF16),
      _row(mix_pre_g[1]), mla_w_dq[0].astype(BF16), _row(mla_q_norm_g[0]), w_uq)

    nq = s // TQ
    assert nq % ATTN_GROUP == 0 and (nq * (nq - 1) // 2) % ATTN_GROUP == 0
    n_slots = 2 * ATTN_GROUP
    items = [(qi, qi) for qi in range(nq)] + [(qi, j) for qi in range(nq) for j in range(qi)]
    qtab = jnp.asarray([qi for qi, _ in items], jnp.int32)
    jtab = jnp.asarray([j for _, j in items], jnp.int32)
    key_chunk = lax.broadcasted_iota(jnp.int32, (TK, TQ), 0) // CHUNK
    qry_chunk = lax.broadcasted_iota(jnp.int32, (TK, TQ), 1) // CHUNK
    bias = jnp.where(key_chunk <= qry_chunk, 0.0, -jnp.inf).astype(F32)
    o_t = pl.pallas_call(
        functools.partial(_attn_kernel, items=tuple(items)),
        grid_spec=pltpu.PrefetchScalarGridSpec(
            num_scalar_prefetch=2,
            grid=(b, N_HEADS),
            in_specs=[pl.BlockSpec((1, 1, nq, KPAD, TQ), lambda i, h, *_: (i, h, 0, 0, 0)),
                      pl.BlockSpec((1, 1, s, KPAD), lambda i, h, *_: (i, h, 0, 0)),
                      pl.BlockSpec((1, 1, s // TK, V_AUG, TK), lambda i, h, *_: (i, h, 0, 0, 0)),
                      _const_spec((TK, TQ))],
            out_specs=pl.BlockSpec((1, 1, nq, V_DIM, TQ), lambda i, h, *_: (i, h, 0, 0, 0)),
            scratch_shapes=[pltpu.VMEM((n_slots, TK, TQ), F32), pltpu.VMEM((n_slots, 1, TQ), F32),
                            pltpu.VMEM((n_slots, TK, TQ), BF16), pltpu.VMEM((n_slots, 1, TQ), F32),
                            pltpu.VMEM((nq, 1, TQ), F32), pltpu.VMEM((nq, V_AUG, TQ), F32)]),
        out_shape=jax.ShapeDtypeStruct((b, N_HEADS, nq, V_DIM, TQ), BF16),
        compiler_params=_params(2),
        name="attention",
    )(qtab, jtab, q_t, k_cat, v_t, bias)

    return pl.pallas_call(
        _out_ffn_kernel,
        grid=(b, n_s),
        in_specs=[pl.BlockSpec((1, N_HEADS, tm // TQ, V_DIM, TQ), lambda i, j: (i, 0, j, 0, 0)), tile,
                  _const_spec((N_HEADS * V_DIM, d)), _const_spec((1, d)), _const_spec((1, d)),
                  _const_spec((d, d_ff)), _const_spec((d_ff, d)), _const_spec((1, d))],
        out_specs=tile,
        out_shape=jax.ShapeDtypeStruct((b, s, d), F32),
        scratch_shapes=[pltpu.VMEM((tm, N_HEADS * V_DIM), BF16)],
        compiler_params=_params(2),
        name="out_ffn",
    )(o_t, h2, mla_w_o[0].astype(BF16), _row(mix_post_g[1]), _row(ffn_pre_g[1]),
      w_ff1[1].astype(BF16), w_ff2[1].astype(BF16), _row(ffn_post_g[1]))
```

```python
import functools

import jax
import jax.numpy as jnp
from jax import lax
from jax.experimental import pallas as pl
from jax.experimental.pallas import tpu as pltpu

F32 = jnp.float32
BF16 = jnp.bfloat16

RMS_EPS = 1e-6
LN_EPS = 1e-5
ROPE_BASE = 10000.0
CHUNK = 64
CONV_WIDTH = 31
N_HEADS = 8
NOPE = 128
ROPE = 64
V_DIM = 128
V_AUG = V_DIM + 16
QK_DIM = NOPE + ROPE
KPAD = 256
LOG2_E = 1.4426950408889634
Q_SCALE = QK_DIM ** -0.5 * LOG2_E

SUBLANES = 8
HALO = 32
CONV_ROWS = 64
CONV_LANES = 128
FFN_COLS = 1024
TQ = 256
TK = 256
ATTN_GROUP = 2
ATTN_UNROLL = 58
VMEM_LIMIT = 56 * 1024 * 1024


def _rms_scale(x):
    return x * lax.rsqrt(jnp.mean(x * x, axis=-1, keepdims=True) + RMS_EPS)


def _ffn(h, g_pre, w1_ref, w2_ref, g_post):
    hn = (_rms_scale(h) * g_pre).astype(BF16)
    d_ff = w1_ref.shape[1]
    acc = None
    for c in range(d_ff // FFN_COLS):
        cols = slice(c * FFN_COLS, (c + 1) * FFN_COLS)
        a = jnp.maximum(jnp.dot(hn, w1_ref[:, cols], preferred_element_type=F32), 0.0)
        t = jnp.dot((a * a).astype(BF16), w2_ref[cols, :], preferred_element_type=F32)
        acc = t if acc is None else acc + t
    return h + _rms_scale(acc) * g_post


def _mixer_ffn_kernel(xa_ref, xb_ref, gpre_ref, wpw1_ref, bpw1_ref, wdw_ref, bdw_ref, lng_ref, lnb_ref,
                      wpw2_ref, bpw2_ref, gmix_ref, gffn_ref, w1_ref, w2_ref, gpost_ref,
                      o_ref, buf_ref, cv_ref, *, tiles_per_seq):
    g = pl.program_id(0)
    tm, d = xa_ref.shape

    @pl.when(g == 0)
    def _no_previous_tile():
        cv_ref[...] = jnp.zeros(cv_ref.shape, F32)

    @pl.when(lax.rem(g, tiles_per_seq) == 0)
    def _zero_history():
        buf_ref[0:HALO, :] = jnp.zeros((HALO, d), F32)

    hn = (_rms_scale(xa_ref[...]) * gpre_ref[...]).astype(BF16)
    u = jnp.dot(hn, wpw1_ref[...], preferred_element_type=F32) + bpw1_ref[...]
    buf_ref[HALO:HALO + tm, :] = u[:, :d] * jax.nn.sigmoid(u[:, d:])

    cv = cv_ref[...]
    xc = cv - jnp.mean(cv, axis=-1, keepdims=True)
    y = xc * lax.rsqrt(jnp.mean(xc * xc, axis=-1, keepdims=True) + LN_EPS) * lng_ref[...] + lnb_ref[...]
    y = (y * jax.nn.sigmoid(y)).astype(BF16)
    m = jnp.dot(y, wpw2_ref[...], preferred_element_type=F32) + bpw2_ref[...]
    h1 = xb_ref[...] + _rms_scale(m) * gmix_ref[...]
    hn1 = (_rms_scale(h1) * gffn_ref[...]).astype(BF16)

    first_off = HALO - (CONV_WIDTH - 1)

    def zero_after(v):
        bits = pltpu.bitcast(v, jnp.int32)
        half = jnp.full(bits.shape, 16, jnp.int32)
        return lax.shift_right_logical(lax.shift_right_logical(bits, half), half).astype(F32)

    def conv_unit(r0, c, pace):
        lanes = slice(c * CONV_LANES, (c + 1) * CONV_LANES)
        out = bdw_ref[:, lanes] + zero_after(pace)
        out = jnp.broadcast_to(out[0:1], (CONV_ROWS, CONV_LANES))
        for r in range(SUBLANES):
            rows = CONV_ROWS + (SUBLANES if r else 0)
            group = None
            for off in range(r, HALO + 1, SUBLANES):
                if off < first_off:
                    continue
                k = off - first_off
                start = r0 + off - r
                term = wdw_ref[k:k + 1, lanes] * buf_ref[start:start + rows, lanes]
                group = term if group is None else group + term
            out = out + group[r:r + CONV_ROWS]
        cv_ref[r0:r0 + CONV_ROWS, lanes] = out

    n_slabs = w1_ref.shape[1] // FFN_COLS
    n_lane_chunks = d // CONV_LANES
    units = [(r0, c) for r0 in range(0, tm, CONV_ROWS) for c in range(n_lane_chunks)]

    per_slab = -(-len(units) // n_slabs)
    acc = None
    for s in range(n_slabs):
        cols = slice(s * FFN_COLS, (s + 1) * FFN_COLS)
        a = jnp.maximum(jnp.dot(hn1, w1_ref[:, cols], preferred_element_type=F32), 0.0)
        t = jnp.dot((a * a).astype(BF16), w2_ref[cols, :], preferred_element_type=F32)
        acc = t if acc is None else acc + t
        slab_units = units[s * per_slab:(s + 1) * per_slab]
        for i, (r0, c) in enumerate(slab_units):
            src = a if 2 * i < len(slab_units) else t
            row = ((2 * i) % len(slab_units)) * (tm // len(slab_units))
            conv_unit(r0, c, src[row:row + SUBLANES, 0:CONV_LANES])
    o_ref[...] = h1 + _rms_scale(acc) * gpost_ref[...]
    buf_ref[0:HALO, :] = buf_ref[tm:tm + HALO, :]


def _proj_kernel(h_ref, cs_ref, gkv_ref, wdkv_ref, gckv_ref, wuk_ref, wuv_ref,
                 gq_ref, wdq_ref, gcq_ref, wuq_ref, k_ref, vt_ref, qt_ref):
    tm = h_ref.shape[1]
    hs = _rms_scale(h_ref[0])
    cs = cs_ref[...]
    c_rank = gckv_ref.shape[1]
    lane = lax.broadcasted_iota(jnp.int32, (tm, 2 * ROPE), 1)

    ckv = jnp.dot((hs * gkv_ref[...]).astype(BF16), wdkv_ref[...], preferred_element_type=F32)
    c_kv = (_rms_scale(ckv[:, :c_rank]) * gckv_ref[...]).astype(BF16)
    kr = ckv[:, c_rank:] * cs
    kr = kr + pltpu.roll(kr, ROPE, 1)
    kr = jnp.where(lane < ROPE, kr, 0.0).astype(BF16)
    kn = jnp.dot(c_kv, wuk_ref[...], preferred_element_type=F32)
    v = jnp.dot(c_kv, wuv_ref[...], preferred_element_type=F32)
    ones_row = (lax.broadcasted_iota(jnp.int32, (V_AUG - V_DIM, TK), 0) == 0).astype(BF16)
    for hd in range(N_HEADS):
        k_ref[0, hd, :, 0:NOPE] = kn[:, hd * NOPE:(hd + 1) * NOPE].astype(BF16)
        k_ref[0, hd, :, NOPE:KPAD] = kr
        for blk in range(tm // TK):
            vh = v[blk * TK:(blk + 1) * TK, hd * V_DIM:(hd + 1) * V_DIM]
            vt_ref[0, hd, blk, 0:V_DIM] = vh.T.astype(BF16)
            vt_ref[0, hd, blk, V_DIM:V_AUG] = ones_row

    cq = jnp.dot((hs * gq_ref[...]).astype(BF16), wdq_ref[...], preferred_element_type=F32)
    cq = (_rms_scale(cq) * gcq_ref[...]).astype(BF16)
    q = jnp.dot(cq, wuq_ref[...], preferred_element_type=F32) * Q_SCALE
    for hd in range(N_HEADS):
        qn = q[:, hd * KPAD:hd * KPAD + NOPE]
        qr = q[:, hd * KPAD + NOPE:(hd + 1) * KPAD] * cs
        qr = qr + pltpu.roll(qr, ROPE, 1)
        for blk in range(tm // TQ):
            rows = slice(blk * TQ, (blk + 1) * TQ)
            qt_ref[0, hd, blk, 0:NOPE, :] = qn[rows].T.astype(BF16)
            qt_ref[0, hd, blk, NOPE:KPAD, :] = qr[rows].T.astype(BF16)


def _attn_kernel(qtab_ref, jtab_ref, qt_ref, k_ref, vt_ref, bias_ref, ot_ref,
                 s_ref, mx_ref, p_ref, al_ref, m_ref, acc_ref, *, items):
    nq = qt_ref.shape[2]
    n_items = qtab_ref.shape[0]
    n_groups = n_items // ATTN_GROUP
    n_diag_groups = nq // ATTN_GROUP

    def item(t):
        if isinstance(t, int):
            return items[t]
        return qtab_ref[t], jtab_ref[t]

    def qk(t, slot, first):
        qi, j = item(t)
        row0 = j * TK if isinstance(j, int) else pl.multiple_of(j * TK, TK)
        k = k_ref[0, 0, pl.ds(row0, TK), :]
        s = jnp.dot(k, qt_ref[0, 0, qi], preferred_element_type=F32)
        if first:
            s = s + bias_ref[...]
        s_ref[slot] = s
        mx_ref[slot] = jnp.max(s, axis=0, keepdims=True)

    def softmax(t, slot, first):
        qi, _ = item(t)
        if first:
            m_new = mx_ref[slot]
            p = jnp.exp2(s_ref[slot] - m_new)
        else:
            m_old = m_ref[qi]
            m_new = jnp.maximum(m_old, mx_ref[slot])
            alpha = jnp.exp2(m_old - m_new)
            p = jnp.exp2(s_ref[slot] - m_new)
            al_ref[slot] = alpha
        m_ref[qi] = m_new
        p_ref[slot] = p.astype(BF16)

    def pv(t, slot, first):
        qi, j = item(t)
        pv_t = jnp.dot(vt_ref[0, 0, j], p_ref[slot], preferred_element_type=F32)
        acc_ref[qi] = pv_t if first else al_ref[slot] * acc_ref[qi] + pv_t

    def body(g, parity, diag_qk, diag_pv, diag_sm):
        stages = ((qk, g, parity, diag_qk), (pv, g - 2, parity, diag_pv), (softmax, g - 1, 1 - parity, diag_sm))
        for stage, group, par, diag in stages:
            if diag is not None:
                for i in range(ATTN_GROUP):
                    stage(group * ATTN_GROUP + i, par * ATTN_GROUP + i, diag)

    def static_body(g):
        flag = lambda grp: (grp < n_diag_groups) if 0 <= grp < n_groups else None
        body(g, g % 2, flag(g), flag(g - 2), flag(g - 1))

    loop_lo = n_diag_groups + 2
    n_iters = max(n_groups - loop_lo, 0) // ATTN_UNROLL
    loop_hi = loop_lo + ATTN_UNROLL * n_iters
    for g in range(0, min(loop_lo, n_groups + 2)):
        static_body(g)

    def body_run(u, carry):
        for i in range(ATTN_UNROLL):
            body(loop_lo + ATTN_UNROLL * u + i, (loop_lo + i) % 2, False, False, False)
        return carry

    if n_iters == 1:
        for g in range(loop_lo, loop_hi):
            static_body(g)
    elif n_iters:
        lax.fori_loop(0, n_iters, body_run, 0)
    for g in range(max(loop_hi, loop_lo), n_groups + 2):
        static_body(g)

    def finish(qi, carry):
        acc = acc_ref[qi]
        ot_ref[0, 0, qi] = (acc[0:V_DIM] * (1.0 / acc[V_DIM:V_DIM + 1])).astype(BF16)
        return carry

    lax.fori_loop(0, nq, finish, 0)


def _out_ffn_kernel(ot_ref, h_ref, wo_ref, gmix_ref, gpre_ref, w1_ref, w2_ref, gpost_ref, out_ref, o_scr):
    for hd in range(ot_ref.shape[1]):
        for blk in range(ot_ref.shape[2]):
            o_t = ot_ref[0, hd, blk].astype(F32)
            o_scr[blk * TQ:(blk + 1) * TQ, hd * V_DIM:(hd + 1) * V_DIM] = o_t.T.astype(BF16)
    m = jnp.dot(o_scr[...], wo_ref[...], preferred_element_type=F32)
    h = h_ref[0] + _rms_scale(m) * gmix_ref[...]
    out_ref[0] = _ffn(h, gpre_ref[...], w1_ref, w2_ref, gpost_ref[...])


def _const_spec(shape):
    zeros = (0,) * len(shape)
    return pl.BlockSpec(shape, lambda *_: zeros, pipeline_mode=pl.Buffered(1))


def _row(v):
    return v.reshape(1, -1).astype(F32)


def _rot_half_cols(w):
    half = w.shape[-1] // 2
    return jnp.concatenate([-w[..., half:], w[..., :half]], axis=-1)


def _params(n_grid_axes, flags=None):
    return pltpu.CompilerParams(dimension_semantics=("arbitrary",) * n_grid_axes,
                                vmem_limit_bytes=VMEM_LIMIT, flags=flags)


def kernel(x, mix_pre_g, mix_post_g, ffn_pre_g, ffn_post_g, w_ff1, w_ff2, conv_w_pw1, conv_b_pw1, conv_w_dw, conv_b_dw, conv_ln_g, conv_ln_b, conv_w_pw2, conv_b_pw2, mla_w_dq, mla_q_norm_g, mla_w_uq, mla_w_o, kv_in_g, kv_w_dkv, kv_norm_g, kv_w_kr, kv_w_uk, kv_w_uv):
    b, s, d = x.shape
    assert mix_pre_g.shape[0] == 2 and conv_w_pw1.shape[0] == 1 and mla_w_dq.shape[0] == 1
    assert conv_w_dw.shape[1] == CONV_WIDTH
    tm = min(512, s)
    assert s % tm == 0 and tm % TQ == 0 and TQ == TK and d % CONV_LANES == 0
    n_s = s // tm
    d_ff = w_ff1.shape[2]
    q_rank = mla_w_dq.shape[2]
    c_rank = kv_w_dkv.shape[1]

    tile = pl.BlockSpec((1, tm, d), lambda i, j: (i, j, 0))

    n_t = b * n_s
    x2 = x.reshape(n_t * tm, d)
    h2 = pl.pallas_call(
        functools.partial(_mixer_ffn_kernel, tiles_per_seq=n_s),
        grid=(n_t + 1,),
        in_specs=[pl.BlockSpec((tm, d), lambda g: (jnp.minimum(g, n_t - 1), 0)),
                  pl.BlockSpec((tm, d), lambda g: (jnp.maximum(g - 1, 0), 0)),
                  _const_spec((1, d)), _const_spec((d, 2 * d)), _const_spec((1, 2 * d)),
                  _const_spec((CONV_WIDTH, d)), _const_spec((1, d)), _const_spec((1, d)),
                  _const_spec((1, d)), _const_spec((d, d)), _const_spec((1, d)), _const_spec((1, d)),
                  _const_spec((1, d)), _const_spec((d, d_ff)), _const_spec((d_ff, d)), _const_spec((1, d))],
        out_specs=pl.BlockSpec((tm, d), lambda g: (jnp.maximum(g - 1, 0), 0)),
        out_shape=jax.ShapeDtypeStruct((n_t * tm, d), F32),
        scratch_shapes=[pltpu.VMEM((tm + HALO, d), F32), pltpu.VMEM((tm, d), F32)],
        compiler_params=_params(1),
        name="mixer_ffn",
    )(x2, x2, _row(mix_pre_g[0]), conv_w_pw1[0].astype(BF16), _row(conv_b_pw1[0]),
      conv_w_dw[0].astype(F32), _row(conv_b_dw[0]), _row(conv_ln_g[0]), _row(conv_ln_b[0]),
      conv_w_pw2[0].astype(BF16), _row(conv_b_pw2[0]), _row(mix_post_g[0]),
      _row(ffn_pre_g[0]), w_ff1[0].astype(BF16), w_ff2[0].astype(BF16), _row(ffn_post_g[0]))
    h2 = h2.reshape(b, s, d)

    inv = 1.0 / (ROPE_BASE ** (jnp.arange(0, ROPE, 2, dtype=F32) / ROPE))
    ang = jnp.arange(s, dtype=F32)[:, None] * inv[None, :]
    cs = jnp.concatenate([jnp.cos(ang), jnp.cos(ang), jnp.sin(ang), jnp.sin(ang)], axis=-1)

    w_dkv_kr = jnp.concatenate([kv_w_dkv, kv_w_kr, _rot_half_cols(kv_w_kr)], axis=-1).astype(BF16)
    w_uq = mla_w_uq[0].reshape(q_rank, N_HEADS, QK_DIM)
    w_uq = jnp.concatenate([w_uq, _rot_half_cols(w_uq[..., NOPE:])], axis=-1)
    w_uq = w_uq.reshape(q_rank, N_HEADS * KPAD).astype(BF16)

    k_cat, v_t, q_t = pl.pallas_call(
        _proj_kernel,
        grid=(b, n_s),
        in_specs=[tile, pl.BlockSpec((tm, 2 * ROPE), lambda i, j: (j, 0)),
                  _const_spec((1, d)), _const_spec((d, c_rank + 2 * ROPE)), _const_spec((1, c_rank)),
                  _const_spec((c_rank, N_HEADS * NOPE)), _const_spec((c_rank, N_HEADS * V_DIM)),
                  _const_spec((1, d)), _const_spec((d, q_rank)), _const_spec((1, q_rank)),
                  _const_spec((q_rank, N_HEADS * KPAD))],
        out_specs=[pl.BlockSpec((1, N_HEADS, tm, KPAD), lambda i, j: (i, 0, j, 0)),
                   pl.BlockSpec((1, N_HEADS, tm // TK, V_AUG, TK), lambda i, j: (i, 0, j, 0, 0)),
                   pl.BlockSpec((1, N_HEADS, tm // TQ, KPAD, TQ), lambda i, j: (i, 0, j, 0, 0))],
        out_shape=[jax.ShapeDtypeStruct((b, N_HEADS, s, KPAD), BF16),
                   jax.ShapeDtypeStruct((b, N_HEADS, s // TK, V_AUG, TK), BF16),
                   jax.ShapeDtypeStruct((b, N_HEADS, s // TQ, KPAD, TQ), BF16)],
        compiler_params=_params(2),
        name="proj",
    )(h2, cs, _row(kv_in_g), w_dkv_kr, _row(kv_norm_g), kv_w_uk.astype(BF16), kv_w_uv.astype(BF16),
      _row(mix_pre_g[1]), mla_w_dq[0].astype(BF16), _row(mla_q_norm_g[0]), w_uq)

    nq = s // TQ
    assert nq % ATTN_GROUP == 0 and (nq * (nq - 1) // 2) % ATTN_GROUP == 0
    n_slots = 2 * ATTN_GROUP
    items = [(qi, qi) for qi in range(nq)] + [(qi, j) for qi in range(nq) for j in range(qi)]
    qtab = jnp.asarray([qi for qi, _ in items], jnp.int32)
    jtab = jnp.asarray([j for _, j in items], jnp.int32)
    key_chunk = lax.broadcasted_iota(jnp.int32, (TK, TQ), 0) // CHUNK
    qry_chunk = lax.broadcasted_iota(jnp.int32, (TK, TQ), 1) // CHUNK
    bias = jnp.where(key_chunk <= qry_chunk, 0.0, -jnp.inf).astype(F32)
    o_t = pl.pallas_call(
        functools.partial(_attn_kernel, items=tuple(items)),
        grid_spec=pltpu.PrefetchScalarGridSpec(
            num_scalar_prefetch=2,
            grid=(b, N_HEADS),
            in_specs=[pl.BlockSpec((1, 1, nq, KPAD, TQ), lambda i, h, *_: (i, h, 0, 0, 0)),
                      pl.BlockSpec((1, 1, s, KPAD), lambda i, h, *_: (i, h, 0, 0)),
                      pl.BlockSpec((1, 1, s // TK, V_AUG, TK), lambda i, h, *_: (i, h, 0, 0, 0)),
                      _const_spec((TK, TQ))],
            out_specs=pl.BlockSpec((1, 1, nq, V_DIM, TQ), lambda i, h, *_: (i, h, 0, 0, 0)),
            scratch_shapes=[pltpu.VMEM((n_slots, TK, TQ), F32), pltpu.VMEM((n_slots, 1, TQ), F32),
                            pltpu.VMEM((n_slots, TK, TQ), BF16), pltpu.VMEM((n_slots, 1, TQ), F32),
                            pltpu.VMEM((nq, 1, TQ), F32), pltpu.VMEM((nq, V_AUG, TQ), F32)]),
        out_shape=jax.ShapeDtypeStruct((b, N_HEADS, nq, V_DIM, TQ), BF16),
        compiler_params=_params(2),
        name="attention",
    )(qtab, jtab, q_t, k_cat, v_t, bias)

    return pl.pallas_call(
        _out_ffn_kernel,
        grid=(b, n_s),
        in_specs=[pl.BlockSpec((1, N_HEADS, tm // TQ, V_DIM, TQ), lambda i, j: (i, 0, j, 0, 0)), tile,
                  _const_spec((N_HEADS * V_DIM, d)), _const_spec((1, d)), _const_spec((1, d)),
                  _const_spec((d, d_ff)), _const_spec((d_ff, d)), _const_spec((1, d))],
        out_specs=tile,
        out_shape=jax.ShapeDtypeStruct((b, s, d), F32),
        scratch_shapes=[pltpu.VMEM((tm, N_HEADS * V_DIM), BF16)],
        compiler_params=_params(2),
        name="out_ffn",
    )(o_t, h2, mla_w_o[0].astype(BF16), _row(mix_post_g[1]), _row(ffn_pre_g[1]),
      w_ff1[1].astype(BF16), w_ff2[1].astype(BF16), _row(ffn_post_g[1]))
```
